```python
import math
import jax, jax.numpy as jnp
from jax import lax
import numpy as np

D_MODEL = 1024
BATCH = 32
SEQ = 2048
DEPTH = 2

GRID_W = 64
CTX_LEN = 256
EPS = 1e-6

NA_HEADS = 8
NA_HEAD_DIM = 64
NA_WIDTH = NA_HEADS * NA_HEAD_DIM
NA_WIN_H = 8
NA_WIN_W = 16
NA_COL_BLOCK = 16
NA_KEY_COLS = 2 * NA_WIN_W
NA_SCALE = NA_HEAD_DIM ** -0.5

MLA_HEADS = 8
MLA_NOPE_DIM = 64
MLA_ROPE_DIM = 32
MLA_V_DIM = 64
MLA_WIDTH = MLA_HEADS * MLA_V_DIM
MLA_Q_RANK = 256
MLA_KV_RANK = 128
MLA_Q_BLOCK = 128
MLA_SCALE = (MLA_NOPE_DIM + MLA_ROPE_DIM) ** -0.5
ROPE_THETA = 10000.0

D_MIX = NA_WIDTH + MLA_WIDTH
IN_SPLITS = (NA_WIDTH, NA_WIDTH, NA_WIDTH, NA_WIDTH,
             MLA_Q_RANK, MLA_KV_RANK, MLA_ROPE_DIM, MLA_WIDTH)
D_IN = sum(IN_SPLITS)

kernel_name = "hybrid_natten_mla_dit_prefix"


def rmsnorm(x, g):
    xf = x.astype(jnp.float32)
    y = xf * lax.rsqrt(jnp.mean(xf * xf, axis=-1, keepdims=True) + EPS)
    return (y * g.astype(jnp.float32)).astype(x.dtype)


def axial_rope_tables(n_tokens):
    t = jnp.arange(n_tokens)
    row = (t // GRID_W).astype(jnp.float32)
    col = (t % GRID_W).astype(jnp.float32)
    per_axis = MLA_ROPE_DIM // 2
    inv = 1.0 / (ROPE_THETA ** (jnp.arange(0, per_axis, 2, dtype=jnp.float32) / per_axis))
    ang = jnp.concatenate([row[:, None] * inv[None], col[:, None] * inv[None]], axis=-1)
    return jnp.cos(ang), jnp.sin(ang)


def apply_rope(x, cos, sin):
    xp = x.reshape(x.shape[:-1] + (MLA_ROPE_DIM // 2, 2))
    x1, x2 = xp[..., 0], xp[..., 1]
    bshape = (cos.shape[0],) + (1,) * (x.ndim - 3) + (cos.shape[1],)
    cs = cos.reshape(bshape).astype(x.dtype)
    sn = sin.reshape(bshape).astype(x.dtype)
    return jnp.stack([x1 * cs - x2 * sn, x1 * sn + x2 * cs], axis=-1).reshape(x.shape)


def project(h, w_in, q_norm_g, w_uq, kv_norm_g, w_ukv):
    B, T, _ = h.shape
    cuts = [int(i) for i in np.cumsum(IN_SPLITS)[:-1]]
    na_q, na_k, na_v, na_gate, c_q, c_kv, k_rope, mla_gate = jnp.split(h @ w_in, cuts, axis=-1)
    na_q = na_q.reshape(B, T, NA_HEADS, NA_HEAD_DIM)
    na_k = na_k.reshape(B, T, NA_HEADS, NA_HEAD_DIM)
    na_v = na_v.reshape(B, T, NA_HEADS, NA_HEAD_DIM)
    q = (rmsnorm(c_q, q_norm_g) @ w_uq).reshape(B, T, MLA_HEADS, MLA_NOPE_DIM + MLA_ROPE_DIM)
    q_nope, q_rope = q[..., :MLA_NOPE_DIM], q[..., MLA_NOPE_DIM:]
    kv = (rmsnorm(c_kv, kv_norm_g) @ w_ukv).reshape(B, T, MLA_HEADS, MLA_NOPE_DIM + MLA_V_DIM)
    k_nope, mla_v = kv[..., :MLA_NOPE_DIM], kv[..., MLA_NOPE_DIM:]
    return na_q, na_k, na_v, na_gate, q_nope, q_rope, k_nope, k_rope, mla_v, mla_gate


def dense_attention(q, k, v, scale):
    s = jnp.einsum('bqhd,bkhd->bhqk', q, k).astype(jnp.float32) * scale
    p = jax.nn.softmax(s, axis=-1)
    return jnp.einsum('bhqk,bkhd->bqhd', p.astype(v.dtype), v)


def neighbourhood_attention(q, k, v, k_ctx, v_ctx, rpb):
    B, S, H, D = q.shape
    rows = S // GRID_W
    wh = min(NA_WIN_H, rows)
    n_cb = GRID_W // NA_COL_BLOCK
    qg = q.reshape(B, rows, n_cb, NA_COL_BLOCK, H, D)
    kg = k.reshape(B, rows, GRID_W, H, D)
    vg = v.reshape(B, rows, GRID_W, H, D)
    q_cols = np.arange(GRID_W).reshape(n_cb, NA_COL_BLOCK)
    q_cs = np.clip(q_cols - NA_WIN_W // 2, 0, GRID_W - NA_WIN_W)
    cb_start = np.clip(np.arange(n_cb) * NA_COL_BLOCK - NA_WIN_W // 2, 0, GRID_W - NA_KEY_COLS)
    key_cols = cb_start[:, None] + np.arange(NA_KEY_COLS)[None]
    kc = key_cols[:, None, :]
    col_mask = (kc >= q_cs[:, :, None]) & (kc < q_cs[:, :, None] + NA_WIN_W)
    dcol = np.clip(kc - q_cols[:, :, None] + NA_WIN_W - 1, 0, 2 * NA_WIN_W - 2)
    bias_col = rpb[:, :, dcol]
    n_loc = wh * NA_KEY_COLS

    def row_block(r):
        rs = jnp.clip(r - wh // 2, 0, rows - wh)
        q_r = lax.dynamic_index_in_dim(qg, r, axis=1, keepdims=False)
        k_r = lax.dynamic_slice_in_dim(kg, rs, wh, axis=1)
        v_r = lax.dynamic_slice_in_dim(vg, rs, wh, axis=1)
        k_w = k_r[:, :, key_cols]
        v_w = v_r[:, :, key_cols]
        s_loc = jnp.einsum('bjqhd,bwjkhd->bhjqwk', q_r, k_w).astype(jnp.float32) * NA_SCALE
        drow = rs + jnp.arange(wh) - r + NA_WIN_H - 1
        bias = jnp.take(bias_col, drow, axis=1).transpose(0, 2, 3, 1, 4)
        s_loc = s_loc + bias[None].astype(jnp.float32)
        s_loc = jnp.where(col_mask[None, None, :, :, None, :], s_loc, -1e30)
        s_ctx = jnp.einsum('bjqhd,bchd->bhjqc', q_r, k_ctx).astype(jnp.float32) * NA_SCALE
        s = jnp.concatenate([s_loc.reshape(B, H, n_cb, NA_COL_BLOCK, n_loc), s_ctx], axis=-1)
        p = jax.nn.softmax(s, axis=-1).astype(v.dtype)
        p_loc = p[..., :n_loc].reshape(B, H, n_cb, NA_COL_BLOCK, wh, NA_KEY_COLS)
        p_ctx = p[..., n_loc:]
        return (jnp.einsum('bhjqwk,bwjkhd->bjqhd', p_loc, v_w)
                + jnp.einsum('bhjqc,bchd->bjqhd', p_ctx, v_ctx))

    out = lax.map(row_block, jnp.arange(rows))
    return jnp.moveaxis(out, 0, 1).reshape(B, S, H, D)


def mla_attend(q_nope, q_rope, k_nope, k_rope, v):
    s = (jnp.einsum('bqhd,bkhd->bhqk', q_nope, k_nope)
         + jnp.einsum('bqhr,bkr->bhqk', q_rope, k_rope)).astype(jnp.float32) * MLA_SCALE
    p = jax.nn.softmax(s, axis=-1)
    return jnp.einsum('bhqk,bkhd->bqhd', p.astype(v.dtype), v)


def mla_latent(q_nope, q_rope, k_nope, k_rope, v, kc_nope, kc_rope, vc):
    B, S, H, _ = q_nope.shape
    nb = S // MLA_Q_BLOCK
    kn = jnp.concatenate([k_nope, kc_nope], axis=1)
    kr = jnp.concatenate([k_rope, kc_rope], axis=1)
    vv = jnp.concatenate([v, vc], axis=1)
    qn = q_nope.reshape(B, nb, MLA_Q_BLOCK, H, MLA_NOPE_DIM).swapaxes(0, 1)
    qr = q_rope.reshape(B, nb, MLA_Q_BLOCK, H, MLA_ROPE_DIM).swapaxes(0, 1)
    o = lax.map(lambda qs: mla_attend(qs[0], qs[1], kn, kr, vv), (qn, qr))
    return o.swapaxes(0, 1).reshape(B, S, H, MLA_V_DIM)


def hybrid_layer(x, ctx, c, c_ctx, norm_g, w_ada, b_ada, w_in, rpb, q_norm_g, w_uq,
                 kv_norm_g, w_ukv, w_out, cos, sin, update_ctx):
    B, S, D = x.shape
    shift, scale, gate = jnp.split(jax.nn.silu(c) @ w_ada + b_ada, 3, axis=-1)
    shift_c, scale_c, gate_c = jnp.split(jax.nn.silu(c_ctx) @ w_ada + b_ada, 3, axis=-1)
    hx = rmsnorm(x, norm_g) * (1.0 + scale[:, None]) + shift[:, None]
    hc = rmsnorm(ctx, norm_g) * (1.0 + scale_c) + shift_c

    (na_q, na_k, na_v, na_gate, q_nope, q_rope, k_nope, k_rope, mla_v, mla_gate) = project(
        hx, w_in, q_norm_g, w_uq, kv_norm_g, w_ukv)
    (cna_q, cna_k, cna_v, cna_gate, cq_nope, cq_rope, ck_nope, ck_rope, cmla_v, cmla_gate) = project(
        hc, w_in, q_norm_g, w_uq, kv_norm_g, w_ukv)
    q_rope = apply_rope(q_rope, cos, sin)
    k_rope = apply_rope(k_rope, cos, sin)

    na_out = neighbourhood_attention(na_q, na_k, na_v, cna_k, cna_v, rpb)
    mla_out = mla_latent(q_nope, q_rope, k_nope, k_rope, mla_v, ck_nope, ck_rope, cmla_v)
    y = jnp.concatenate([na_out.reshape(B, S, NA_WIDTH) * jax.nn.silu(na_gate),
                         mla_out.reshape(B, S, MLA_WIDTH) * jax.nn.silu(mla_gate)], axis=-1) @ w_out
    x = x + gate[:, None] * y

    if update_ctx:
        C = ctx.shape[1]
        cna_out = dense_attention(cna_q, cna_k, cna_v, NA_SCALE)
        cmla_out = mla_attend(cq_nope, cq_rope, ck_nope, ck_rope, cmla_v)
        yc = jnp.concatenate([cna_out.reshape(B, C, NA_WIDTH) * jax.nn.silu(cna_gate),
                              cmla_out.reshape(B, C, MLA_WIDTH) * jax.nn.silu(cmla_gate)], axis=-1) @ w_out
        ctx = ctx + gate_c * yc
    return x, ctx


def setup_inputs(seed: int = 0) -> dict:
    key = jax.random.key(seed)
    ks = jax.random.split(key, 16)
    f32 = jnp.float32
    nrm = lambda k, shape, s: jax.random.normal(k, shape, f32) * s
    return {
        "x": nrm(ks[0], (BATCH, SEQ, D_MODEL), 1.0),
        "c": nrm(ks[1], (BATCH, D_MODEL), 1.0),
        "ctx": nrm(ks[2], (BATCH, CTX_LEN, D_MODEL), 1.0),
        "c_ctx": nrm(ks[3], (D_MODEL,), 1.0),
        "norm_g": 1.0 + nrm(ks[4], (DEPTH, D_MODEL), 0.01),
        "w_ada": nrm(ks[5], (DEPTH, D_MODEL, 3 * D_MODEL), D_MODEL ** -0.5),
        "b_ada": nrm(ks[6], (DEPTH, 3 * D_MODEL), 0.01),
        "w_in": nrm(ks[7], (DEPTH, D_MODEL, D_IN), D_MODEL ** -0.5),
        "na_rpb": nrm(ks[8], (DEPTH, NA_HEADS, 2 * NA_WIN_H - 1, 2 * NA_WIN_W - 1), 0.1),
        "q_norm_g": 1.0 + nrm(ks[9], (DEPTH, MLA_Q_RANK), 0.01),
        "w_uq": nrm(ks[10], (DEPTH, MLA_Q_RANK, MLA_HEADS * (MLA_NOPE_DIM + MLA_ROPE_DIM)), MLA_Q_RANK ** -0.5),
        "kv_norm_g": 1.0 + nrm(ks[11], (DEPTH, MLA_KV_RANK), 0.01),
        "w_ukv": nrm(ks[12], (DEPTH, MLA_KV_RANK, MLA_HEADS * (MLA_NOPE_DIM + MLA_V_DIM)), MLA_KV_RANK ** -0.5),
        "w_out": nrm(ks[13], (DEPTH, D_MIX, D_MODEL), D_MIX ** -0.5),
        "final_norm_g": 1.0 + nrm(ks[14], (D_MODEL,), 0.01),
    }


def reference(x, c, ctx, c_ctx, norm_g, w_ada, b_ada, w_in, na_rpb, q_norm_g, w_uq,
              kv_norm_g, w_ukv, w_out, final_norm_g):
    cos, sin = axial_rope_tables(x.shape[1])
    for l in range(DEPTH):
        x, ctx = hybrid_layer(x, ctx, c, c_ctx, norm_g[l], w_ada[l], b_ada[l], w_in[l], na_rpb[l],
                              q_norm_g[l], w_uq[l], kv_norm_g[l], w_ukv[l], w_out[l], cos, sin,
                              update_ctx=(l < DEPTH - 1))
    return rmsnorm(x, final_norm_g)
```

```python
import functools
import math

import numpy as np
import jax
import jax.numpy as jnp
from jax import lax
from jax.experimental import pallas as pl
from jax.experimental.pallas import tpu as pltpu

D_MODEL = 1024
SEQ = 2048
CTX = 256
TOK = SEQ + CTX
GRID_W = 64
EPS = 1e-6
DEPTH = 2

HEADS = 8
NA_DIM = 64
NA_WIDTH = HEADS * NA_DIM
NA_WIN_H = 8
NA_WIN_W = 16
MLA_NOPE = 64
MLA_ROPE = 32
MLA_V = 64
MLA_WIDTH = HEADS * MLA_V
Q_RANK = 256
KV_RANK = 128
ROPE_THETA = 10000.0
LOG2E = math.log2(math.e)
NA_QSCALE = NA_DIM ** -0.5 * LOG2E
MLA_QSCALE = (MLA_NOPE + MLA_ROPE) ** -0.5 * LOG2E
MASK_VALUE = -1e30

TILE = 256
N_LAT_TILES = SEQ // TILE
N_TILES = TOK // TILE
HEAD_PAD = 128
ONES_ROWS = 16
NA_KEY_TILES = 3
NA_KEYS = NA_KEY_TILES * TILE + CTX

VMEM_LIMIT_BYTES = 48 * 1024 * 1024

BF16 = jnp.bfloat16
F32 = jnp.float32

_NT = (((1,), (1,)), ((), ()))
_TN = (((0,), (0,)), ((), ()))


def _params(*semantics):
    return pltpu.CompilerParams(dimension_semantics=semantics, vmem_limit_bytes=VMEM_LIMIT_BYTES)


def _rms(v, g):
    return v * lax.rsqrt(jnp.mean(v * v, axis=-1, keepdims=True) + EPS) * g


def _ada_kernel(c_ref, w_ref, b_ref, o_ref):
    cv = c_ref[...]
    a = (cv * jax.nn.sigmoid(cv)).astype(BF16)
    o_ref[0] = jnp.dot(a, w_ref[0].astype(BF16), preferred_element_type=F32) + b_ref[0]


def _ada_call(cc, w_ada, b_ada):
    rows = cc.shape[0]
    return pl.pallas_call(
        _ada_kernel,
        grid=(DEPTH, 3),
        in_specs=[
            pl.BlockSpec((rows, D_MODEL), lambda l, j: (0, 0)),
            pl.BlockSpec((1, D_MODEL, D_MODEL), lambda l, j: (l, 0, j)),
            pl.BlockSpec((1, 1, D_MODEL), lambda l, j: (l, 0, j)),
        ],
        out_specs=pl.BlockSpec((1, rows, D_MODEL), lambda l, j: (l, 0, j)),
        out_shape=jax.ShapeDtypeStruct((DEPTH, rows, 3 * D_MODEL), F32),
        compiler_params=_params("arbitrary", "arbitrary"),
        name="ada_modulation",
    )(cc, w_ada, b_ada.reshape(DEPTH, 1, 3 * D_MODEL))


def _proj_kernel(x_ref, mod_ref, ng_ref, wtok_ref, wt_ref, qg_ref, wq_ref, kvg_ref, wk_ref, wv_ref,
                 ck_ref, sa_ref, sb_ref, cos_ref, sin_ref,
                 kna_ref, qna_ref, vna_ref, gt_ref, qml_ref, kml_ref, vml_ref):
    x = x_ref[0]
    shift = mod_ref[0, 0, 0:1, :]
    scale = mod_ref[0, 0, 1:2, :]
    h = _rms(x, ng_ref[...]) * (1.0 + scale) + shift
    hb = h.astype(BF16)
    r_tok = jnp.dot(hb, wtok_ref[...], preferred_element_type=F32)
    r_t = lax.dot_general(wt_ref[...], hb, _NT, preferred_element_type=F32)

    kna_ref[0] = r_tok[:, 0:NA_WIDTH].astype(BF16)
    qna_ref[0, 0] = (r_t[0:NA_WIDTH] * NA_QSCALE).astype(BF16)
    vna_ref[0, 0] = r_t[NA_WIDTH:2 * NA_WIDTH].astype(BF16)
    gates = r_t[2 * NA_WIDTH:]
    gt_ref[0, 0] = (gates * jax.nn.sigmoid(gates)).astype(BF16)

    c0 = NA_WIDTH
    cq = _rms(r_tok[:, c0:c0 + Q_RANK], qg_ref[...]).astype(BF16)
    ckv = _rms(r_tok[:, c0 + Q_RANK:c0 + Q_RANK + KV_RANK], kvg_ref[...]).astype(BF16)
    kr = r_tok[:, c0 + Q_RANK + KV_RANK:]

    q_t = lax.dot_general(wq_ref[...], cq, _NT, preferred_element_type=F32) * MLA_QSCALE
    cos_t = cos_ref[...]
    sin_t = sin_ref[...]
    half = MLA_ROPE // 2
    for hd in range(HEADS):
        r0 = hd * HEAD_PAD
        x1 = q_t[r0 + MLA_NOPE:r0 + MLA_NOPE + half]
        x2 = q_t[r0 + MLA_NOPE + half:r0 + MLA_NOPE + MLA_ROPE]
        qml_ref[0, 0, r0:r0 + MLA_NOPE, :] = q_t[r0:r0 + MLA_NOPE].astype(BF16)
        qml_ref[0, 0, r0 + MLA_NOPE:r0 + MLA_NOPE + half, :] = (x1 * cos_t - x2 * sin_t).astype(BF16)
        qml_ref[0, 0, r0 + MLA_NOPE + half:r0 + MLA_NOPE + MLA_ROPE, :] = (x1 * sin_t + x2 * cos_t).astype(BF16)
        qml_ref[0, 0, r0 + MLA_NOPE + MLA_ROPE:r0 + HEAD_PAD, :] = q_t[r0 + MLA_NOPE + MLA_ROPE:r0 + HEAD_PAD].astype(BF16)

    krr = (kr * ck_ref[...] + pltpu.roll(kr, HEAD_PAD - half, 1) * sa_ref[...]
           + pltpu.roll(kr, half, 1) * sb_ref[...]).astype(BF16)
    lhs = jnp.concatenate([ckv, krr], axis=1)
    kml_ref[0] = jnp.dot(lhs, wk_ref[...], preferred_element_type=F32).astype(BF16)
    vml_ref[0] = lax.dot_general(wv_ref[...], ckv, _NT, preferred_element_type=F32).astype(BF16)


def _proj_call(xc, mod, ng, wtok, wt, qg, wq, kvg, wk, wv, tabs):
    b = xc.shape[0]
    ck, sa, sb, cos_t, sin_t = tabs
    const = lambda shape: pl.BlockSpec(shape, lambda i, t: (0,) * len(shape))
    tile4 = lambda rows: pl.BlockSpec((1, 1, rows, TILE), lambda i, t: (i, t, 0, 0))
    return pl.pallas_call(
        _proj_kernel,
        grid=(b, N_TILES),
        in_specs=[
            pl.BlockSpec((1, TILE, D_MODEL), lambda i, t: (i, t, 0)),
            pl.BlockSpec((1, 1, 3, D_MODEL), lambda i, t: (i, t // N_LAT_TILES, 0, 0)),
            const((1, D_MODEL)),
            const((D_MODEL, 1024)),
            const((2048, D_MODEL)),
            const((1, Q_RANK)),
            const((HEADS * HEAD_PAD, Q_RANK)),
            const((1, KV_RANK)),
            const((2 * KV_RANK, HEADS * HEAD_PAD)),
            const((MLA_WIDTH, KV_RANK)),
            pl.BlockSpec((TILE, HEAD_PAD), lambda i, t: (t, 0)),
            pl.BlockSpec((TILE, HEAD_PAD), lambda i, t: (t, 0)),
            pl.BlockSpec((TILE, HEAD_PAD), lambda i, t: (t, 0)),
            pl.BlockSpec((MLA_ROPE // 2, TILE), lambda i, t: (0, t)),
            pl.BlockSpec((MLA_ROPE // 2, TILE), lambda i, t: (0, t)),
        ],
        out_specs=[
            pl.BlockSpec((1, TILE, NA_WIDTH), lambda i, t: (i, t, 0)),
            tile4(NA_WIDTH),
            tile4(NA_WIDTH),
            tile4(NA_WIDTH + MLA_WIDTH),
            tile4(HEADS * HEAD_PAD),
            pl.BlockSpec((1, TILE, HEADS * HEAD_PAD), lambda i, t: (i, t, 0)),
            pl.BlockSpec((1, MLA_WIDTH, TILE), lambda i, t: (i, 0, t)),
        ],
        out_shape=[
            jax.ShapeDtypeStruct((b, TOK, NA_WIDTH), BF16),
            jax.ShapeDtypeStruct((b, N_TILES, NA_WIDTH, TILE), BF16),
            jax.ShapeDtypeStruct((b, N_TILES, NA_WIDTH, TILE), BF16),
            jax.ShapeDtypeStruct((b, N_TILES, NA_WIDTH + MLA_WIDTH, TILE), BF16),
            jax.ShapeDtypeStruct((b, N_TILES, HEADS * HEAD_PAD, TILE), BF16),
            jax.ShapeDtypeStruct((b, TOK, HEADS * HEAD_PAD), BF16),
            jax.ShapeDtypeStruct((b, MLA_WIDTH, TOK), BF16),
        ],
        compiler_params=_params("arbitrary", "arbitrary"),
        name="norm_projections",
    )(xc, mod, ng, wtok, wt, qg, wq, kvg, wk, wv, ck, sa, sb, cos_t, sin_t)


def _with_ones(vt):
    return jnp.concatenate([vt, jnp.ones((ONES_ROWS, vt.shape[1]), vt.dtype)], axis=0)


def _attend_t(s, v_aug, gate):
    m = jnp.max(s, axis=0, keepdims=True)
    p = jnp.exp2(s - m).astype(BF16)
    o = jnp.dot(v_aug, p, preferred_element_type=F32)
    dv = gate.shape[0]
    return (o[:dv] * (1.0 / o[dv:dv + 1]) * gate.astype(F32)).astype(BF16)


def _mla_kernel(q_ref, k_ref, v_ref, g_ref, o_ref, *, out_tiles):
    k = k_ref[0]
    v_aug = _with_ones(v_ref[0])

    def latent_tile(qi, carry):
        s = jnp.dot(k, q_ref[0, qi], preferred_element_type=F32)
        o_ref[0, qi] = _attend_t(s, v_aug, g_ref[0, qi])
        return carry

    lax.fori_loop(0, N_LAT_TILES, latent_tile, 0)
    if out_tiles == N_TILES:
        qi = N_LAT_TILES
        s = jnp.dot(k[SEQ:], q_ref[0, qi], preferred_element_type=F32)
        o_ref[0, qi] = _attend_t(s, v_aug[:, SEQ:], g_ref[0, qi])


def _mla_call(qml, kml, vml, gt, out_tiles):
    b = qml.shape[0]
    return pl.pallas_call(
        functools.partial(_mla_kernel, out_tiles=out_tiles),
        grid=(b, HEADS),
        in_specs=[
            pl.BlockSpec((1, N_TILES, HEAD_PAD, TILE), lambda i, h: (i, 0, h, 0)),
            pl.BlockSpec((1, TOK, HEAD_PAD), lambda i, h: (i, 0, h)),
            pl.BlockSpec((1, MLA_V, TOK), lambda i, h: (i, h, 0)),
            pl.BlockSpec((1, N_TILES, MLA_V, TILE), lambda i, h: (i, 0, HEADS + h, 0)),
        ],
        out_specs=pl.BlockSpec((1, out_tiles, MLA_V, TILE), lambda i, h: (i, 0, h, 0)),
        out_shape=jax.ShapeDtypeStruct((b, out_tiles, MLA_WIDTH, TILE), BF16),
        compiler_params=_params("arbitrary", "arbitrary"),
        name="mla_attention",
    )(qml, kml, vml, gt)


def _na_kernel(q_ref, k_ref, v_ref, g_ref, b_ref, o_ref, *, out_tiles):
    hd = pl.program_id(0)
    row = lax.broadcasted_iota(jnp.int32, (2 * NA_DIM, 1), 0)
    own = (row < NA_DIM) == (hd % 2 == 0)
    k_ctx = k_ref[0, SEQ:, :]
    v_ctx = _with_ones(v_ref[0, N_LAT_TILES])

    def q_tile(qi):
        qp = q_ref[0, qi]
        return jnp.where(own, qp, jnp.zeros_like(qp))

    def latent_tile(qi, carry):
        t0 = jnp.clip(qi - 1, 0, N_LAT_TILES - NA_KEY_TILES)
        var = jnp.where(qi == 0, 0, jnp.where(qi == N_LAT_TILES - 1, 2, 1))
        k_loc = k_ref[0, pl.ds(pl.multiple_of(t0 * TILE, TILE), NA_KEY_TILES * TILE), :]
        k_sel = jnp.concatenate([k_loc, k_ctx], axis=0)
        s = jnp.dot(k_sel, q_tile(qi), preferred_element_type=F32) + b_ref[var, 0]
        v_sel = jnp.concatenate(
            [_with_ones(v_ref[0, t0 + j]) for j in range(NA_KEY_TILES)] + [v_ctx], axis=1)
        o_ref[0, qi] = _attend_t(s, v_sel, g_ref[0, qi])
        return carry

    lax.fori_loop(0, N_LAT_TILES, latent_tile, 0)
    if out_tiles == N_TILES:
        qi = N_LAT_TILES
        s = jnp.dot(k_ctx, q_tile(qi), preferred_element_type=F32)
        o_ref[0, qi] = _attend_t(s, v_ctx, g_ref[0, qi])


def _na_call(qna, kna, vna, gt, bias_t, out_tiles):
    b = qna.shape[0]
    return pl.pallas_call(
        functools.partial(_na_kernel, out_tiles=out_tiles),
        grid=(HEADS, b),
        in_specs=[
            pl.BlockSpec((1, N_TILES, 2 * NA_DIM, TILE), lambda h, i: (i, 0, h // 2, 0)),
            pl.BlockSpec((1, TOK, 2 * NA_DIM), lambda h, i: (i, 0, h // 2)),
            pl.BlockSpec((1, N_TILES, NA_DIM, TILE), lambda h, i: (i, 0, h, 0)),
            pl.BlockSpec((1, N_TILES, NA_DIM, TILE), lambda h, i: (i, 0, h, 0)),
            pl.BlockSpec((3, 1, NA_KEYS, TILE), lambda h, i: (0, h, 0, 0)),
        ],
        out_specs=pl.BlockSpec((1, out_tiles, NA_DIM, TILE), lambda h, i: (i, 0, h, 0)),
        out_shape=jax.ShapeDtypeStruct((b, out_tiles, NA_WIDTH, TILE), BF16),
        compiler_params=_params("arbitrary", "arbitrary"),
        name="na_attention",
    )(qna, kna, vna, gt, bias_t)


def _out_kernel(ona_ref, oml_ref, wa_ref, wb_ref, x_ref, mod_ref, fg_ref, o_ref, *, final):
    y = (lax.dot_general(ona_ref[0, 0], wa_ref[...], _TN, preferred_element_type=F32)
         + lax.dot_general(oml_ref[0, 0], wb_ref[...], _TN, preferred_element_type=F32))
    xn = x_ref[0] + mod_ref[0, 0, 2:3, :] * y
    if final:
        xn = _rms(xn, fg_ref[...])
    o_ref[0] = xn


def _out_call(ona, oml, wa, wb, xc, mod, fg, final):
    b = xc.shape[0]
    tiles = ona.shape[1]
    const = lambda shape: pl.BlockSpec(shape, lambda i, t: (0,) * len(shape))
    return pl.pallas_call(
        functools.partial(_out_kernel, final=final),
        grid=(b, tiles),
        in_specs=[
            pl.BlockSpec((1, 1, NA_WIDTH, TILE), lambda i, t: (i, t, 0, 0)),
            pl.BlockSpec((1, 1, MLA_WIDTH, TILE), lambda i, t: (i, t, 0, 0)),
            const((NA_WIDTH, D_MODEL)),
            const((MLA_WIDTH, D_MODEL)),
            pl.BlockSpec((1, TILE, D_MODEL), lambda i, t: (i, t, 0)),
            pl.BlockSpec((1, 1, 3, D_MODEL), lambda i, t: (i, t // N_LAT_TILES, 0, 0)),
            const((1, D_MODEL)),
        ],
        out_specs=pl.BlockSpec((1, TILE, D_MODEL), lambda i, t: (i, t, 0)),
        out_shape=jax.ShapeDtypeStruct((b, tiles * TILE, D_MODEL), F32),
        compiler_params=_params("arbitrary", "arbitrary"),
        name="out_projection",
    )(ona, oml, wa, wb, xc, mod, fg)


def _rope_tables():
    t = np.arange(SEQ)
    row = (t // GRID_W).astype(np.float32)
    col = (t % GRID_W).astype(np.float32)
    per_axis = MLA_ROPE // 2
    inv = (1.0 / (ROPE_THETA ** (jnp.arange(0, per_axis, 2, dtype=F32) / per_axis)))
    ang = jnp.concatenate([jnp.asarray(row)[:, None] * inv[None], jnp.asarray(col)[:, None] * inv[None]], axis=-1)
    cos = jnp.concatenate([jnp.cos(ang), jnp.ones((CTX, per_axis), F32)], axis=0)
    sin = jnp.concatenate([jnp.sin(ang), jnp.zeros((CTX, per_axis), F32)], axis=0)
    zeros = jnp.zeros((TOK, HEAD_PAD - 2 * per_axis), F32)
    zh = jnp.zeros((TOK, per_axis), F32)
    ck = jnp.concatenate([cos, cos, zeros], axis=1)
    sa = jnp.concatenate([-sin, zh, zeros], axis=1)
    sb = jnp.concatenate([zh, sin, zeros], axis=1)
    return ck, sa, sb, cos.T, sin.T


def _na_bias(rpb):
    rows = SEQ // GRID_W
    tile_rows = TILE // GRID_W
    out = []
    for g in (0, 1, N_LAT_TILES - 1):
        t0 = min(max(g - 1, 0), N_LAT_TILES - NA_KEY_TILES)
        q_r = g * tile_rows + np.arange(TILE) // GRID_W
        q_c = np.arange(TILE) % GRID_W
        k_r = t0 * tile_rows + np.arange(NA_KEY_TILES * TILE) // GRID_W
        k_c = np.arange(NA_KEY_TILES * TILE) % GRID_W
        rs = np.clip(q_r - NA_WIN_H // 2, 0, rows - NA_WIN_H)
        cs = np.clip(q_c - NA_WIN_W // 2, 0, GRID_W - NA_WIN_W)
        ok = ((k_r[:, None] >= rs[None]) & (k_r[:, None] < rs[None] + NA_WIN_H)
              & (k_c[:, None] >= cs[None]) & (k_c[:, None] < cs[None] + NA_WIN_W))
        drow = np.clip(k_r[:, None] - q_r[None] + NA_WIN_H - 1, 0, 2 * NA_WIN_H - 2)
        dcol = np.clip(k_c[:, None] - q_c[None] + NA_WIN_W - 1, 0, 2 * NA_WIN_W - 2)
        loc = jnp.where(ok[None], rpb[:, drow, dcol] * LOG2E, MASK_VALUE)
        out.append(jnp.concatenate([loc, jnp.zeros((HEADS, CTX, TILE), F32)], axis=1))
    return jnp.stack(out, axis=0)


def _layer_weights(w_in, w_uq, w_ukv, w_out):
    cuts = np.cumsum([NA_WIDTH] * 4 + [Q_RANK, KV_RANK, MLA_ROPE, MLA_WIDTH])
    na_q, na_k, na_v, na_g, c_q, c_kv, k_r, ml_g = jnp.split(w_in, [int(c) for c in cuts[:-1]], axis=1)
    half = MLA_ROPE // 2
    kr_blk = jnp.concatenate([k_r[:, 0::2], k_r[:, 1::2], jnp.zeros((D_MODEL, HEAD_PAD - MLA_ROPE), F32)], axis=1)
    wtok = jnp.concatenate([na_k, c_q, c_kv, kr_blk], axis=1).astype(BF16)
    wt = jnp.concatenate([na_q, na_v, na_g, ml_g], axis=1).T.astype(BF16)

    uq = w_uq.reshape(Q_RANK, HEADS, MLA_NOPE + MLA_ROPE)
    uq = jnp.concatenate([uq[..., :MLA_NOPE], uq[..., MLA_NOPE::2], uq[..., MLA_NOPE + 1::2],
                          jnp.zeros((Q_RANK, HEADS, HEAD_PAD - MLA_NOPE - MLA_ROPE), F32)], axis=-1)
    wq = uq.reshape(Q_RANK, HEADS * HEAD_PAD).T.astype(BF16)

    ukv = w_ukv.reshape(KV_RANK, HEADS, MLA_NOPE + MLA_V)
    k_top = jnp.concatenate([ukv[..., :MLA_NOPE], jnp.zeros((KV_RANK, HEADS, HEAD_PAD - MLA_NOPE), F32)], axis=-1)
    eye = jnp.concatenate([jnp.zeros((MLA_ROPE, MLA_NOPE), F32), jnp.eye(MLA_ROPE, dtype=F32),
                           jnp.zeros((MLA_ROPE, HEAD_PAD - MLA_NOPE - MLA_ROPE), F32)], axis=1)
    k_bot = jnp.concatenate([jnp.broadcast_to(eye[:, None], (MLA_ROPE, HEADS, HEAD_PAD)),
                             jnp.zeros((KV_RANK - MLA_ROPE, HEADS, HEAD_PAD), F32)], axis=0)
    wk = jnp.concatenate([k_top, k_bot], axis=0).reshape(2 * KV_RANK, HEADS * HEAD_PAD).astype(BF16)
    wv = ukv[..., MLA_NOPE:].reshape(KV_RANK, MLA_WIDTH).T.astype(BF16)

    wa = w_out[:NA_WIDTH].astype(BF16)
    wb = w_out[NA_WIDTH:].astype(BF16)
    return wtok, wt, wq, wk, wv, wa, wb


def kernel(x, c, ctx, c_ctx, norm_g, w_ada, b_ada, w_in, na_rpb, q_norm_g, w_uq, kv_norm_g, w_ukv, w_out,
           final_norm_g):
    b = x.shape[0]
    tabs = _rope_tables()

    rows = -(-(b + 1) // 8) * 8
    cc = jnp.concatenate([c, c_ctx[None], jnp.zeros((rows - b - 1, D_MODEL), F32)], axis=0)
    ada = _ada_call(cc, w_ada, b_ada)

    xc = jnp.concatenate([x, ctx], axis=1)
    out = None
    for l in range(DEPTH):
        last = l == DEPTH - 1
        lat = ada[l, :b].reshape(b, 1, 3, D_MODEL)
        con = jnp.broadcast_to(ada[l, b].reshape(1, 1, 3, D_MODEL), (b, 1, 3, D_MODEL))
        mod = jnp.concatenate([lat, con], axis=1)
        wtok, wt, wq, wk, wv, wa, wb = _layer_weights(w_in[l], w_uq[l], w_ukv[l], w_out[l])
        kna, qna, vna, gt, qml, kml, vml = _proj_call(
            xc, mod, norm_g[l].reshape(1, D_MODEL), wtok, wt, q_norm_g[l].reshape(1, Q_RANK), wq,
            kv_norm_g[l].reshape(1, KV_RANK), wk, wv, tabs)
        out_tiles = N_LAT_TILES if last else N_TILES
        ona = _na_call(qna, kna, vna, gt, _na_bias(na_rpb[l]), out_tiles)
        oml = _mla_call(qml, kml, vml, gt, out_tiles)
        res = _out_call(ona, oml, wa, wb, xc, mod, final_norm_g.reshape(1, D_MODEL), last)
        if last:
            out = res
        else:
            xc = res
    return out
```

```python
import functools
import math

import numpy as np
import jax
import jax.numpy as jnp
from jax import lax
from jax.experimental import pallas as pl
from jax.experimental.pallas import tpu as pltpu

D_MODEL = 1024
SEQ = 2048
CTX = 256
TOK = SEQ + CTX
GRID_W = 64
EPS = 1e-6
DEPTH = 2

HEADS = 8
NA_DIM = 64
NA_WIDTH = HEADS * NA_DIM
NA_WIN_H = 8
NA_WIN_W = 16
MLA_NOPE = 64
MLA_ROPE = 32
MLA_V = 64
MLA_WIDTH = HEADS * MLA_V
Q_RANK = 256
KV_RANK = 128
ROPE_THETA = 10000.0
LOG2E = math.log2(math.e)
NA_QSCALE = NA_DIM ** -0.5 * LOG2E
MLA_QSCALE = (MLA_NOPE + MLA_ROPE) ** -0.5 * LOG2E
MASK_VALUE = -1e30

TILE = 256
N_LAT_TILES = SEQ // TILE
N_TILES = TOK // TILE
HEAD_PAD = 128
ONES_ROWS = 16
NA_KEY_TILES = 3
NA_KEYS = NA_KEY_TILES * TILE + CTX
RPB_ROWS_PAD = 16
RPB_COLS_PAD = 32

VMEM_LIMIT_BYTES = 48 * 1024 * 1024

BF16 = jnp.bfloat16
F32 = jnp.float32

_NT = (((1,), (1,)), ((), ()))
_TN = (((0,), (0,)), ((), ()))


def _params(*semantics):
    return pltpu.CompilerParams(dimension_semantics=semantics, vmem_limit_bytes=VMEM_LIMIT_BYTES)


def _rms(v, g):
    return v * lax.rsqrt(jnp.mean(v * v, axis=-1, keepdims=True) + EPS) * g


def _ada_kernel(c_ref, w_ref, b_ref, o_ref):
    cv = c_ref[...]
    a = (cv * jax.nn.sigmoid(cv)).astype(BF16)
    o_ref[0] = jnp.dot(a, w_ref[0].astype(BF16), preferred_element_type=F32) + b_ref[0]


def _ada_call(cc, w_ada, b_ada):
    rows = cc.shape[0]
    return pl.pallas_call(
        _ada_kernel,
        grid=(DEPTH, 3),
        in_specs=[
            pl.BlockSpec((rows, D_MODEL), lambda l, j: (0, 0)),
            pl.BlockSpec((1, D_MODEL, D_MODEL), lambda l, j: (l, 0, j)),
            pl.BlockSpec((1, 1, D_MODEL), lambda l, j: (l, 0, j)),
        ],
        out_specs=pl.BlockSpec((1, rows, D_MODEL), lambda l, j: (l, 0, j)),
        out_shape=jax.ShapeDtypeStruct((DEPTH, rows, 3 * D_MODEL), F32),
        compiler_params=_params("arbitrary", "arbitrary"),
        name="ada_modulation",
    )(cc, w_ada, b_ada.reshape(DEPTH, 1, 3 * D_MODEL))


def _proj_kernel(x_ref, mod_ref, ng_ref, wtok_ref, wt_ref, qg_ref, wq_ref, kvg_ref, wk_ref, wv_ref,
                 ck_ref, sa_ref, sb_ref, cos_ref, sin_ref,
                 kna_ref, qna_ref, vna_ref, gt_ref, qml_ref, kml_ref, vml_ref):
    x = x_ref[0]
    shift = mod_ref[0, 0, 0:1, :]
    scale = mod_ref[0, 0, 1:2, :]
    h = _rms(x, ng_ref[...]) * (1.0 + scale) + shift
    hb = h.astype(BF16)
    r_tok = jnp.dot(hb, wtok_ref[...], preferred_element_type=F32)
    r_t = lax.dot_general(wt_ref[...], hb, _NT, preferred_element_type=F32)

    kna_ref[0] = r_tok[:, 0:NA_WIDTH].astype(BF16)
    qna_ref[0, 0] = (r_t[0:NA_WIDTH] * NA_QSCALE).astype(BF16)
    vna_ref[0, 0] = r_t[NA_WIDTH:2 * NA_WIDTH].astype(BF16)
    gates = r_t[2 * NA_WIDTH:]
    gt_ref[0, 0] = (gates * jax.nn.sigmoid(gates)).astype(BF16)

    c0 = NA_WIDTH
    cq = _rms(r_tok[:, c0:c0 + Q_RANK], qg_ref[...]).astype(BF16)
    ckv = _rms(r_tok[:, c0 + Q_RANK:c0 + Q_RANK + KV_RANK], kvg_ref[...]).astype(BF16)
    kr = r_tok[:, c0 + Q_RANK + KV_RANK:]

    q_t = lax.dot_general(wq_ref[...], cq, _NT, preferred_element_type=F32) * MLA_QSCALE
    cos_t = cos_ref[...]
    sin_t = sin_ref[...]
    half = MLA_ROPE // 2
    for hd in range(HEADS):
        r0 = hd * HEAD_PAD
        x1 = q_t[r0 + MLA_NOPE:r0 + MLA_NOPE + half]
        x2 = q_t[r0 + MLA_NOPE + half:r0 + MLA_NOPE + MLA_ROPE]
        qml_ref[0, 0, r0:r0 + MLA_NOPE, :] = q_t[r0:r0 + MLA_NOPE].astype(BF16)
        qml_ref[0, 0, r0 + MLA_NOPE:r0 + MLA_NOPE + half, :] = (x1 * cos_t - x2 * sin_t).astype(BF16)
        qml_ref[0, 0, r0 + MLA_NOPE + half:r0 + MLA_NOPE + MLA_ROPE, :] = (x1 * sin_t + x2 * cos_t).astype(BF16)
        qml_ref[0, 0, r0 + MLA_NOPE + MLA_ROPE:r0 + HEAD_PAD, :] = q_t[r0 + MLA_NOPE + MLA_ROPE:r0 + HEAD_PAD].astype(BF16)

    krr = (kr * ck_ref[...] + pltpu.roll(kr, HEAD_PAD - half, 1) * sa_ref[...]
           + pltpu.roll(kr, half, 1) * sb_ref[...]).astype(BF16)
    lhs = jnp.concatenate([ckv, krr], axis=1)
    kml_ref[0] = jnp.dot(lhs, wk_ref[...], preferred_element_type=F32).astype(BF16)
    vml_ref[0] = lax.dot_general(wv_ref[...], ckv, _NT, preferred_element_type=F32).astype(BF16)


def _proj_call(xc, mod, ng, wtok, wt, qg, wq, kvg, wk, wv, tabs):
    b = xc.shape[0]
    ck, sa, sb, cos_t, sin_t = tabs
    const = lambda shape: pl.BlockSpec(shape, lambda i, t: (0,) * len(shape))
    tile4 = lambda rows: pl.BlockSpec((1, 1, rows, TILE), lambda i, t: (i, t, 0, 0))
    return pl.pallas_call(
        _proj_kernel,
        grid=(b, N_TILES),
        in_specs=[
            pl.BlockSpec((1, TILE, D_MODEL), lambda i, t: (i, t, 0)),
            pl.BlockSpec((1, 1, 3, D_MODEL), lambda i, t: (i, t // N_LAT_TILES, 0, 0)),
            const((1, D_MODEL)),
            const((D_MODEL, 1024)),
            const((2048, D_MODEL)),
            const((1, Q_RANK)),
            const((HEADS * HEAD_PAD, Q_RANK)),
            const((1, KV_RANK)),
            const((2 * KV_RANK, HEADS * HEAD_PAD)),
            const((MLA_WIDTH, KV_RANK)),
            pl.BlockSpec((TILE, HEAD_PAD), lambda i, t: (t, 0)),
            pl.BlockSpec((TILE, HEAD_PAD), lambda i, t: (t, 0)),
            pl.BlockSpec((TILE, HEAD_PAD), lambda i, t: (t, 0)),
            pl.BlockSpec((MLA_ROPE // 2, TILE), lambda i, t: (0, t)),
            pl.BlockSpec((MLA_ROPE // 2, TILE), lambda i, t: (0, t)),
        ],
        out_specs=[
            pl.BlockSpec((1, TILE, NA_WIDTH), lambda i, t: (i, t, 0)),
            tile4(NA_WIDTH),
            tile4(NA_WIDTH),
            tile4(NA_WIDTH + MLA_WIDTH),
            tile4(HEADS * HEAD_PAD),
            pl.BlockSpec((1, TILE, HEADS * HEAD_PAD), lambda i, t: (i, t, 0)),
            pl.BlockSpec((1, MLA_WIDTH, TILE), lambda i, t: (i, 0, t)),
        ],
        out_shape=[
            jax.ShapeDtypeStruct((b, TOK, NA_WIDTH), BF16),
            jax.ShapeDtypeStruct((b, N_TILES, NA_WIDTH, TILE), BF16),
            jax.ShapeDtypeStruct((b, N_TILES, NA_WIDTH, TILE), BF16),
            jax.ShapeDtypeStruct((b, N_TILES, NA_WIDTH + MLA_WIDTH, TILE), BF16),
            jax.ShapeDtypeStruct((b, N_TILES, HEADS * HEAD_PAD, TILE), BF16),
            jax.ShapeDtypeStruct((b, TOK, HEADS * HEAD_PAD), BF16),
            jax.ShapeDtypeStruct((b, MLA_WIDTH, TOK), BF16),
        ],
        compiler_params=_params("arbitrary", "arbitrary"),
        name="norm_projections",
    )(xc, mod, ng, wtok, wt, qg, wq, kvg, wk, wv, ck, sa, sb, cos_t, sin_t)


def _with_ones(vt):
    return jnp.concatenate([vt, jnp.ones((ONES_ROWS, vt.shape[1]), vt.dtype)], axis=0)


def _store_scores(s, s_scr, m_scr):
    s_scr[0:s.shape[0], :] = s
    m_scr[...] = jnp.max(s, axis=0, keepdims=True)


def _probs(s_scr, m_scr, keys):
    return jnp.exp2(s_scr[0:keys, :] - m_scr[...]).astype(BF16)


def _finish(v_aug, p, gate):
    o = jnp.dot(v_aug, p, preferred_element_type=F32)
    dv = gate.shape[0]
    return (o[:dv] * (1.0 / o[dv:dv + 1]) * gate.astype(F32)).astype(BF16)


def _pipeline(n, scores, finish, bufs):
    scores(0, *bufs[0])
    for j in range(n):
        if j + 1 < n:
            scores(j + 1, *bufs[(j + 1) % 2])
        finish(j, *bufs[j % 2])


def _mla_kernel(q_ref, k_ref, v_ref, g_ref, o_ref, sa_ref, ma_ref, sb_ref, mb_ref, *, out_tiles):
    v_aug = _with_ones(v_ref[0])
    n_steps = N_LAT_TILES // 2

    def scores(j, s_scr, m_scr):
        q2 = jnp.concatenate([q_ref[0, 2 * j], q_ref[0, 2 * j + 1]], axis=1)
        _store_scores(jnp.dot(k_ref[0], q2, preferred_element_type=F32), s_scr, m_scr)

    def finish(j, s_scr, m_scr):
        p = _probs(s_scr, m_scr, TOK)
        for i in range(2):
            o_ref[0, 2 * j + i] = _finish(v_aug, p[:, i * TILE:(i + 1) * TILE], g_ref[0, 2 * j + i])

    _pipeline(n_steps, scores, finish, ((sa_ref, ma_ref), (sb_ref, mb_ref)))

    if out_tiles == N_TILES:
        qi = N_LAT_TILES
        s = jnp.dot(k_ref[0, SEQ:, :], q_ref[0, qi], preferred_element_type=F32)
        p = jnp.exp2(s - jnp.max(s, axis=0, keepdims=True)).astype(BF16)
        o_ref[0, qi] = _finish(v_aug[:, SEQ:], p, g_ref[0, qi])


def _mla_call(qml, kml, vml, gt, out_tiles):
    b = qml.shape[0]
    return pl.pallas_call(
        functools.partial(_mla_kernel, out_tiles=out_tiles),
        grid=(b, HEADS),
        in_specs=[
            pl.BlockSpec((1, N_TILES, HEAD_PAD, TILE), lambda i, h: (i, 0, h, 0)),
            pl.BlockSpec((1, TOK, HEAD_PAD), lambda i, h: (i, 0, h)),
            pl.BlockSpec((1, MLA_V, TOK), lambda i, h: (i, h, 0)),
            pl.BlockSpec((1, N_TILES, MLA_V, TILE), lambda i, h: (i, 0, HEADS + h, 0)),
        ],
        out_specs=pl.BlockSpec((1, out_tiles, MLA_V, TILE), lambda i, h: (i, 0, h, 0)),
        out_shape=jax.ShapeDtypeStruct((b, out_tiles, MLA_WIDTH, TILE), BF16),
        scratch_shapes=[pltpu.VMEM((TOK, 2 * TILE), F32), pltpu.VMEM((1, 2 * TILE), F32),
                        pltpu.VMEM((TOK, 2 * TILE), F32), pltpu.VMEM((1, 2 * TILE), F32)],
        compiler_params=_params("arbitrary", "arbitrary"),
        name="mla_attention",
    )(qml, kml, vml, gt)


def _na_kernel(q_ref, k_ref, v_ref, g_ref, b_ref, o_ref, sa_ref, ma_ref, sb_ref, mb_ref, *, out_tiles):
    row = lax.broadcasted_iota(jnp.int32, (2 * NA_DIM, 1), 0)
    first = row < NA_DIM

    def q_pair(qi):
        qp = q_ref[0, qi]
        zero = jnp.zeros_like(qp)
        return jnp.concatenate([jnp.where(first, qp, zero), jnp.where(first, zero, qp)], axis=1)

    def v_pair(tiles):
        return [jnp.concatenate([_with_ones(v_ref[0, t, i * NA_DIM:(i + 1) * NA_DIM, :]) for t in tiles], axis=1)
                for i in range(2)]

    def window(qi):
        t0 = min(max(qi - 1, 0), N_LAT_TILES - NA_KEY_TILES)
        var = 0 if qi == 0 else (2 if qi == N_LAT_TILES - 1 else 1)
        return t0, var

    def gated(qi, vs, p):
        g = g_ref[0, qi]
        o_ref[0, qi] = jnp.concatenate(
            [_finish(vs[i], p[:, i * TILE:(i + 1) * TILE], g[i * NA_DIM:(i + 1) * NA_DIM]) for i in range(2)], axis=0)

    def scores(qi, s_scr, m_scr):
        t0, var = window(qi)
        k_sel = jnp.concatenate([k_ref[0, t0 * TILE:(t0 + NA_KEY_TILES) * TILE, :], k_ref[0, SEQ:, :]], axis=0)
        bias = jnp.concatenate([b_ref[var, 0], b_ref[var, 1]], axis=1)
        _store_scores(jnp.dot(k_sel, q_pair(qi), preferred_element_type=F32) + bias, s_scr, m_scr)

    def finish(qi, s_scr, m_scr):
        t0, _ = window(qi)
        gated(qi, v_pair(list(range(t0, t0 + NA_KEY_TILES)) + [N_LAT_TILES]), _probs(s_scr, m_scr, NA_KEYS))

    _pipeline(N_LAT_TILES, scores, finish, ((sa_ref, ma_ref), (sb_ref, mb_ref)))

    if out_tiles == N_TILES:
        qi = N_LAT_TILES
        s = jnp.dot(k_ref[0, SEQ:, :], q_pair(qi), preferred_element_type=F32)
        p = jnp.exp2(s - jnp.max(s, axis=0, keepdims=True)).astype(BF16)
        gated(qi, v_pair([N_LAT_TILES]), p)


def _na_call(qna, kna, vna, gt, bias_t, out_tiles):
    b = qna.shape[0]
    pair = 2 * NA_DIM
    return pl.pallas_call(
        functools.partial(_na_kernel, out_tiles=out_tiles),
        grid=(HEADS // 2, b),
        in_specs=[
            pl.BlockSpec((1, N_TILES, pair, TILE), lambda h, i: (i, 0, h, 0)),
            pl.BlockSpec((1, TOK, pair), lambda h, i: (i, 0, h)),
            pl.BlockSpec((1, N_TILES, pair, TILE), lambda h, i: (i, 0, h, 0)),
            pl.BlockSpec((1, N_TILES, pair, TILE), lambda h, i: (i, 0, h, 0)),
            pl.BlockSpec((3, 2, NA_KEYS, TILE), lambda h, i: (0, h, 0, 0)),
        ],
        out_specs=pl.BlockSpec((1, out_tiles, pair, TILE), lambda h, i: (i, 0, h, 0)),
        out_shape=jax.ShapeDtypeStruct((b, out_tiles, NA_WIDTH, TILE), BF16),
        scratch_shapes=[pltpu.VMEM((NA_KEYS, 2 * TILE), F32), pltpu.VMEM((1, 2 * TILE), F32),
                        pltpu.VMEM((NA_KEYS, 2 * TILE), F32), pltpu.VMEM((1, 2 * TILE), F32)],
        compiler_params=_params("arbitrary", "arbitrary"),
        name="na_attention",
    )(qna, kna, vna, gt, bias_t)


def _out_kernel(ona_ref, oml_ref, wa_ref, wb_ref, x_ref, mod_ref, fg_ref, o_ref, *, final):
    y = (lax.dot_general(ona_ref[0, 0], wa_ref[...], _TN, preferred_element_type=F32)
         + lax.dot_general(oml_ref[0, 0], wb_ref[...], _TN, preferred_element_type=F32))
    xn = x_ref[0] + mod_ref[0, 0, 2:3, :] * y
    if final:
        xn = _rms(xn, fg_ref[...])
    o_ref[0] = xn


def _out_call(ona, oml, wa, wb, xc, mod, fg, final):
    b = xc.shape[0]
    tiles = ona.shape[1]
    const = lambda shape: pl.BlockSpec(shape, lambda i, t: (0,) * len(shape))
    return pl.pallas_call(
        functools.partial(_out_kernel, final=final),
        grid=(b, tiles),
        in_specs=[
            pl.BlockSpec((1, 1, NA_WIDTH, TILE), lambda i, t: (i, t, 0, 0)),
            pl.BlockSpec((1, 1, MLA_WIDTH, TILE), lambda i, t: (i, t, 0, 0)),
            const((NA_WIDTH, D_MODEL)),
            const((MLA_WIDTH, D_MODEL)),
            pl.BlockSpec((1, TILE, D_MODEL), lambda i, t: (i, t, 0)),
            pl.BlockSpec((1, 1, 3, D_MODEL), lambda i, t: (i, t // N_LAT_TILES, 0, 0)),
            const((1, D_MODEL)),
        ],
        out_specs=pl.BlockSpec((1, TILE, D_MODEL), lambda i, t: (i, t, 0)),
        out_shape=jax.ShapeDtypeStruct((b, tiles * TILE, D_MODEL), F32),
        compiler_params=_params("arbitrary", "arbitrary"),
        name="out_projection",
    )(ona, oml, wa, wb, xc, mod, fg)


def _rope_tables():
    t = np.arange(SEQ)
    row = (t // GRID_W).astype(np.float32)
    col = (t % GRID_W).astype(np.float32)
    per_axis = MLA_ROPE // 2
    inv = (1.0 / (ROPE_THETA ** (jnp.arange(0, per_axis, 2, dtype=F32) / per_axis)))
    ang = jnp.concatenate([jnp.asarray(row)[:, None] * inv[None], jnp.asarray(col)[:, None] * inv[None]], axis=-1)
    cos = jnp.concatenate([jnp.cos(ang), jnp.ones((CTX, per_axis), F32)], axis=0)
    sin = jnp.concatenate([jnp.sin(ang), jnp.zeros((CTX, per_axis), F32)], axis=0)
    zeros = jnp.zeros((TOK, HEAD_PAD - 2 * per_axis), F32)
    zh = jnp.zeros((TOK, per_axis), F32)
    ck = jnp.concatenate([cos, cos, zeros], axis=1)
    sa = jnp.concatenate([-sin, zh, zeros], axis=1)
    sb = jnp.concatenate([zh, sin, zeros], axis=1)
    return ck, sa, sb, cos.T, sin.T


def _col_tables():
    k_c = np.arange(GRID_W)[:, None]
    q_c = np.arange(GRID_W)[None, :]
    cs = np.clip(q_c - NA_WIN_W // 2, 0, GRID_W - NA_WIN_W)
    dcol = np.clip(k_c - q_c + NA_WIN_W - 1, 0, 2 * NA_WIN_W - 2).reshape(-1)
    sel = (np.arange(RPB_COLS_PAD)[:, None] == dcol[None, :]).astype(np.float32)
    ok = ((k_c >= cs) & (k_c < cs + NA_WIN_W)).astype(np.float32).reshape(1, -1)
    return jnp.asarray(sel, BF16), jnp.asarray(ok)


def _bias_kernel(rpb_ref, sel_ref, ok_ref, o_ref):
    v = rpb_ref[...] * LOG2E
    hi = v.astype(BF16)
    r1 = v - hi.astype(F32)
    mid = r1.astype(BF16)
    lo = (r1 - mid.astype(F32)).astype(BF16)
    sel = sel_ref[...]
    t = (jnp.dot(hi, sel, preferred_element_type=F32) + jnp.dot(mid, sel, preferred_element_type=F32)
         + jnp.dot(lo, sel, preferred_element_type=F32))
    o_ref[...] = jnp.where(ok_ref[...] > 0.5, t, MASK_VALUE)


def _na_bias(rpb):
    sel, ok = _col_tables()
    pad = jnp.zeros((HEADS, RPB_ROWS_PAD, RPB_COLS_PAD), F32).at[:, :2 * NA_WIN_H - 1, :2 * NA_WIN_W - 1].set(rpb)
    toe = pl.pallas_call(
        _bias_kernel,
        out_shape=jax.ShapeDtypeStruct((HEADS * RPB_ROWS_PAD, GRID_W * GRID_W), F32),
        compiler_params=pltpu.CompilerParams(vmem_limit_bytes=VMEM_LIMIT_BYTES),
        name="na_bias_columns",
    )(pad.reshape(HEADS * RPB_ROWS_PAD, RPB_COLS_PAD), sel, ok)
    toe = toe.reshape(HEADS, RPB_ROWS_PAD, GRID_W, GRID_W)

    rows = SEQ // GRID_W
    tile_rows = TILE // GRID_W
    key_rows = NA_KEY_TILES * tile_rows
    masked = jnp.full((HEADS, GRID_W, GRID_W), MASK_VALUE, F32)
    out = []
    for g in (0, 1, N_LAT_TILES - 1):
        t0 = min(max(g - 1, 0), N_LAT_TILES - NA_KEY_TILES)
        blocks = []
        for kr in range(key_rows):
            k_r = t0 * tile_rows + kr
            row_blocks = []
            for qi in range(tile_rows):
                q_r = g * tile_rows + qi
                rs = min(max(q_r - NA_WIN_H // 2, 0), rows - NA_WIN_H)
                inside = rs <= k_r < rs + NA_WIN_H
                row_blocks.append(toe[:, k_r - q_r + NA_WIN_H - 1] if inside else masked)
            blocks.append(jnp.stack(row_blocks, axis=2))
        loc = jnp.stack(blocks, axis=1).reshape(HEADS, key_rows * GRID_W, TILE)
        out.append(jnp.concatenate([loc, jnp.zeros((HEADS, CTX, TILE), F32)], axis=1))
    return jnp.stack(out, axis=0)


def _layer_weights(w_in, w_uq, w_ukv, w_out):
    cuts = np.cumsum([NA_WIDTH] * 4 + [Q_RANK, KV_RANK, MLA_ROPE, MLA_WIDTH])
    na_q, na_k, na_v, na_g, c_q, c_kv, k_r, ml_g = jnp.split(w_in, [int(c) for c in cuts[:-1]], axis=1)
    half = MLA_ROPE // 2
    kr_blk = jnp.concatenate([k_r[:, 0::2], k_r[:, 1::2], jnp.zeros((D_MODEL, HEAD_PAD - MLA_ROPE), F32)], axis=1)
    wtok = jnp.concatenate([na_k, c_q, c_kv, kr_blk], axis=1).astype(BF16)
    wt = jnp.concatenate([na_q, na_v, na_g, ml_g], axis=1).T.astype(BF16)

    uq = w_uq.reshape(Q_RANK, HEADS, MLA_NOPE + MLA_ROPE)
    uq = jnp.concatenate([uq[..., :MLA_NOPE], uq[..., MLA_NOPE::2], uq[..., MLA_NOPE + 1::2],
                          jnp.zeros((Q_RANK, HEADS, HEAD_PAD - MLA_NOPE - MLA_ROPE), F32)], axis=-1)
    wq = uq.reshape(Q_RANK, HEADS * HEAD_PAD).T.astype(BF16)

    ukv = w_ukv.reshape(KV_RANK, HEADS, MLA_NOPE + MLA_V)
    k_top = jnp.concatenate([ukv[..., :MLA_NOPE], jnp.zeros((KV_RANK, HEADS, HEAD_PAD - MLA_NOPE), F32)], axis=-1)
    eye = jnp.concatenate([jnp.zeros((MLA_ROPE, MLA_NOPE), F32), jnp.eye(MLA_ROPE, dtype=F32),
                           jnp.zeros((MLA_ROPE, HEAD_PAD - MLA_NOPE - MLA_ROPE), F32)], axis=1)
    k_bot = jnp.concatenate([jnp.broadcast_to(eye[:, None], (MLA_ROPE, HEADS, HEAD_PAD)),
                             jnp.zeros((KV_RANK - MLA_ROPE, HEADS, HEAD_PAD), F32)], axis=0)
    wk = jnp.concatenate([k_top, k_bot], axis=0).reshape(2 * KV_RANK, HEADS * HEAD_PAD).astype(BF16)
    wv = ukv[..., MLA_NOPE:].reshape(KV_RANK, MLA_WIDTH).T.astype(BF16)

    wa = w_out[:NA_WIDTH].astype(BF16)
    wb = w_out[NA_WIDTH:].astype(BF16)
    return wtok, wt, wq, wk, wv, wa, wb


def kernel(x, c, ctx, c_ctx, norm_g, w_ada, b_ada, w_in, na_rpb, q_norm_g, w_uq, kv_norm_g, w_ukv, w_out,
           final_norm_g):
    b = x.shape[0]
    tabs = _rope_tables()

    rows = -(-(b + 1) // 8) * 8
    cc = jnp.concatenate([c, c_ctx[None], jnp.zeros((rows - b - 1, D_MODEL), F32)], axis=0)
    ada = _ada_call(cc, w_ada, b_ada)

    xc = jnp.concatenate([x, ctx], axis=1)
    out = None
    for l in range(DEPTH):
        last = l == DEPTH - 1
        lat = ada[l, :b].reshape(b, 1, 3, D_MODEL)
        con = jnp.broadcast_to(ada[l, b].reshape(1, 1, 3, D_MODEL), (b, 1, 3, D_MODEL))
        mod = jnp.concatenate([lat, con], axis=1)
        wtok, wt, wq, wk, wv, wa, wb = _layer_weights(w_in[l], w_uq[l], w_ukv[l], w_out[l])
        kna, qna, vna, gt, qml, kml, vml = _proj_call(
            xc, mod, norm_g[l].reshape(1, D_MODEL), wtok, wt, q_norm_g[l].reshape(1, Q_RANK), wq,
            kv_norm_g[l].reshape(1, KV_RANK), wk, wv, tabs)
        out_tiles = N_LAT_TILES if last else N_TILES
        ona = _na_call(qna, kna, vna, gt, _na_bias(na_rpb[l]), out_tiles)
        oml = _mla_call(qml, kml, vml, gt, out_tiles)
        res = _out_call(ona, oml, wa, wb, xc, mod, final_norm_g.reshape(1, D_MODEL), last)
        if last:
            out = res
        else:
            xc = res
    return out
```

```python
import functools
import math

import numpy as np
import jax
import jax.numpy as jnp
from jax import lax
from jax.experimental import pallas as pl
from jax.experimental.pallas import tpu as pltpu

D_MODEL = 1024
SEQ = 2048
CTX = 256
TOK = SEQ + CTX
GRID_W = 64
EPS = 1e-6
DEPTH = 2

HEADS = 8
NA_DIM = 64
NA_WIDTH = HEADS * NA_DIM
NA_WIN_H = 8
NA_WIN_W = 16
MLA_NOPE = 64
MLA_ROPE = 32
MLA_V = 64
MLA_WIDTH = HEADS * MLA_V
Q_RANK = 256
KV_RANK = 128
ROPE_THETA = 10000.0
LOG2E = math.log2(math.e)
NA_QSCALE = NA_DIM ** -0.5 * LOG2E
MLA_QSCALE = (MLA_NOPE + MLA_ROPE) ** -0.5 * LOG2E
MASK_VALUE = -1e30

TILE = 256
N_LAT_TILES = SEQ // TILE
N_TILES = TOK // TILE
HEAD_PAD = 128
ONES_ROWS = 16
NA_KEY_TILES = 3
NA_LOCAL_KEYS = NA_KEY_TILES * TILE
KEY_CHUNK = 256
RPB_ROWS_PAD = 16
RPB_COLS_PAD = 32

VMEM_LIMIT_BYTES = 48 * 1024 * 1024

BF16 = jnp.bfloat16
F32 = jnp.float32

_NT = (((1,), (1,)), ((), ()))
_TN = (((0,), (0,)), ((), ()))


def _params(*semantics):
    return pltpu.CompilerParams(dimension_semantics=semantics, vmem_limit_bytes=VMEM_LIMIT_BYTES)


def _rms(v, g):
    return v * lax.rsqrt(jnp.mean(v * v, axis=-1, keepdims=True) + EPS) * g


def _ada_kernel(c_ref, w_ref, b_ref, o_ref):
    cv = c_ref[...]
    a = (cv * jax.nn.sigmoid(cv)).astype(BF16)
    o_ref[0] = jnp.dot(a, w_ref[0].astype(BF16), preferred_element_type=F32) + b_ref[0]


def _ada_call(cc, w_ada, b_ada):
    rows = cc.shape[0]
    return pl.pallas_call(
        _ada_kernel,
        grid=(DEPTH, 3),
        in_specs=[
            pl.BlockSpec((rows, D_MODEL), lambda l, j: (0, 0)),
            pl.BlockSpec((1, D_MODEL, D_MODEL), lambda l, j: (l, 0, j)),
            pl.BlockSpec((1, 1, D_MODEL), lambda l, j: (l, 0, j)),
        ],
        out_specs=pl.BlockSpec((1, rows, D_MODEL), lambda l, j: (l, 0, j)),
        out_shape=jax.ShapeDtypeStruct((DEPTH, rows, 3 * D_MODEL), F32),
        compiler_params=_params("arbitrary", "arbitrary"),
        name="ada_modulation",
    )(cc, w_ada, b_ada.reshape(DEPTH, 1, 3 * D_MODEL))


def _stream_specs(con):
    con_block = con.shape[1] // CTX - 1
    return [pl.BlockSpec((1, TILE, D_MODEL), lambda i, t: (i, jnp.minimum(t, N_LAT_TILES - 1), 0)),
            pl.BlockSpec((1, CTX, D_MODEL), lambda i, t: (i, con_block, 0))]


def _stream_tile(lat_ref, con_ref):
    return jnp.where(pl.program_id(1) == N_LAT_TILES, con_ref[0], lat_ref[0])


def _proj_kernel(lat_ref, con_ref, mod_ref, ng_ref, wtok_ref, wt_ref, qg_ref, wq_ref, kvg_ref, wk_ref, wv_ref,
                 ck_ref, sa_ref, sb_ref, cos_ref, sin_ref,
                 kna_ref, qna_ref, vna_ref, gt_ref, qml_ref, kml_ref, vml_ref):
    x = _stream_tile(lat_ref, con_ref)
    shift = mod_ref[0, 0, 0:1, :]
    scale = mod_ref[0, 0, 1:2, :]
    h = _rms(x, ng_ref[...]) * (1.0 + scale) + shift
    hb = h.astype(BF16)
    r_tok = jnp.dot(hb, wtok_ref[...], preferred_element_type=F32)
    r_t = lax.dot_general(wt_ref[...], hb, _NT, preferred_element_type=F32)

    kna_ref[0] = r_tok[:, 0:NA_WIDTH].astype(BF16)
    qna_ref[0, 0] = (r_t[0:NA_WIDTH] * NA_QSCALE).astype(BF16)
    vna_ref[0, 0] = r_t[NA_WIDTH:2 * NA_WIDTH].astype(BF16)
    gates = r_t[2 * NA_WIDTH:]
    gt_ref[0, 0] = (gates * jax.nn.sigmoid(gates)).astype(BF16)

    c0 = NA_WIDTH
    cq = _rms(r_tok[:, c0:c0 + Q_RANK], qg_ref[...]).astype(BF16)
    ckv = _rms(r_tok[:, c0 + Q_RANK:c0 + Q_RANK + KV_RANK], kvg_ref[...]).astype(BF16)
    kr = r_tok[:, c0 + Q_RANK + KV_RANK:]

    q_t = lax.dot_general(wq_ref[...], cq, _NT, preferred_element_type=F32) * MLA_QSCALE
    cos_t = cos_ref[...]
    sin_t = sin_ref[...]
    half = MLA_ROPE // 2
    for hd in range(HEADS):
        r0 = hd * HEAD_PAD
        x1 = q_t[r0 + MLA_NOPE:r0 + MLA_NOPE + half]
        x2 = q_t[r0 + MLA_NOPE + half:r0 + MLA_NOPE + MLA_ROPE]
        qml_ref[0, 0, r0:r0 + MLA_NOPE, :] = q_t[r0:r0 + MLA_NOPE].astype(BF16)
        qml_ref[0, 0, r0 + MLA_NOPE:r0 + MLA_NOPE + half, :] = (x1 * cos_t - x2 * sin_t).astype(BF16)
        qml_ref[0, 0, r0 + MLA_NOPE + half:r0 + MLA_NOPE + MLA_ROPE, :] = (x1 * sin_t + x2 * cos_t).astype(BF16)
        qml_ref[0, 0, r0 + MLA_NOPE + MLA_ROPE:r0 + HEAD_PAD, :] = q_t[r0 + MLA_NOPE + MLA_ROPE:r0 + HEAD_PAD].astype(BF16)

    krr = (kr * ck_ref[...] + pltpu.roll(kr, HEAD_PAD - half, 1) * sa_ref[...]
           + pltpu.roll(kr, half, 1) * sb_ref[...]).astype(BF16)
    lhs = jnp.concatenate([ckv, krr], axis=1)
    kml_ref[0] = jnp.dot(lhs, wk_ref[...], preferred_element_type=F32).astype(BF16)
    vml_ref[0] = lax.dot_general(wv_ref[...], ckv, _NT, preferred_element_type=F32).astype(BF16)


def _proj_call(lat, con, mod, ng, wtok, wt, qg, wq, kvg, wk, wv, tabs):
    b = lat.shape[0]
    ck, sa, sb, cos_t, sin_t = tabs
    const = lambda shape: pl.BlockSpec(shape, lambda i, t: (0,) * len(shape))
    tile4 = lambda rows: pl.BlockSpec((1, 1, rows, TILE), lambda i, t: (i, t, 0, 0))
    return pl.pallas_call(
        _proj_kernel,
        grid=(b, N_TILES),
        in_specs=_stream_specs(con) + [
            pl.BlockSpec((1, 1, 3, D_MODEL), lambda i, t: (i, t // N_LAT_TILES, 0, 0)),
            const((1, D_MODEL)),
            const((D_MODEL, 1024)),
            const((2048, D_MODEL)),
            const((1, Q_RANK)),
            const((HEADS * HEAD_PAD, Q_RANK)),
            const((1, KV_RANK)),
            const((2 * KV_RANK, HEADS * HEAD_PAD)),
            const((MLA_WIDTH, KV_RANK)),
            pl.BlockSpec((TILE, HEAD_PAD), lambda i, t: (t, 0)),
            pl.BlockSpec((TILE, HEAD_PAD), lambda i, t: (t, 0)),
            pl.BlockSpec((TILE, HEAD_PAD), lambda i, t: (t, 0)),
            pl.BlockSpec((MLA_ROPE // 2, TILE), lambda i, t: (0, t)),
            pl.BlockSpec((MLA_ROPE // 2, TILE), lambda i, t: (0, t)),
        ],
        out_specs=[
            pl.BlockSpec((1, TILE, NA_WIDTH), lambda i, t: (i, t, 0)),
            tile4(NA_WIDTH),
            tile4(NA_WIDTH),
            tile4(NA_WIDTH + MLA_WIDTH),
            tile4(HEADS * HEAD_PAD),
            pl.BlockSpec((1, TILE, HEADS * HEAD_PAD), lambda i, t: (i, t, 0)),
            pl.BlockSpec((1, MLA_WIDTH, TILE), lambda i, t: (i, 0, t)),
        ],
        out_shape=[
            jax.ShapeDtypeStruct((b, TOK, NA_WIDTH), BF16),
            jax.ShapeDtypeStruct((b, N_TILES, NA_WIDTH, TILE), BF16),
            jax.ShapeDtypeStruct((b, N_TILES, NA_WIDTH, TILE), BF16),
            jax.ShapeDtypeStruct((b, N_TILES, NA_WIDTH + MLA_WIDTH, TILE), BF16),
            jax.ShapeDtypeStruct((b, N_TILES, HEADS * HEAD_PAD, TILE), BF16),
            jax.ShapeDtypeStruct((b, TOK, HEADS * HEAD_PAD), BF16),
            jax.ShapeDtypeStruct((b, MLA_WIDTH, TOK), BF16),
        ],
        compiler_params=_params("arbitrary", "arbitrary"),
        name="norm_projections",
    )(lat, con, mod, ng, wtok, wt, qg, wq, kvg, wk, wv, ck, sa, sb, cos_t, sin_t)


def _with_ones(vt):
    return jnp.concatenate([vt, jnp.ones((ONES_ROWS, vt.shape[1]), vt.dtype)], axis=0)


def _online_pv(chunks):
    m = o = None
    for s, v in chunks:
        s = s.astype(BF16)
        cm = jnp.max(s, axis=0, keepdims=True)
        if m is None:
            m = cm
            o = jnp.dot(v, jnp.exp2(s - m), preferred_element_type=F32)
        else:
            m_new = jnp.maximum(m, cm)
            alpha = jnp.exp2(m.astype(F32) - m_new.astype(F32))
            o = o * alpha + jnp.dot(v, jnp.exp2(s - m_new), preferred_element_type=F32)
            m = m_new
    return o


def _gated(o, gate):
    dv = gate.shape[0]
    return (o[:dv] * (1.0 / o[dv:dv + 1]) * gate.astype(F32)).astype(BF16)


def _mla_kernel(q_ref, k_ref, v_ref, g_ref, o_ref, *, out_tiles):
    v_aug = _with_ones(v_ref[0])
    chunks = [(a, min(a + KEY_CHUNK, TOK)) for a in range(0, TOK, KEY_CHUNK)]

    for j in range(N_LAT_TILES // 2):
        q2 = jnp.concatenate([q_ref[0, 2 * j], q_ref[0, 2 * j + 1]], axis=1)
        o = _online_pv([(jnp.dot(k_ref[0, a:b, :], q2, preferred_element_type=F32), v_aug[:, a:b])
                        for a, b in chunks])
        for i in range(2):
            o_ref[0, 2 * j + i] = _gated(o[:, i * TILE:(i + 1) * TILE], g_ref[0, 2 * j + i])

    if out_tiles == N_TILES:
        qi = N_LAT_TILES
        s = jnp.dot(k_ref[0, SEQ:, :], q_ref[0, qi], preferred_element_type=F32)
        o_ref[0, qi] = _gated(_online_pv([(s, v_aug[:, SEQ:])]), g_ref[0, qi])


def _mla_call(qml, kml, vml, gt, out_tiles):
    b = qml.shape[0]
    return pl.pallas_call(
        functools.partial(_mla_kernel, out_tiles=out_tiles),
        grid=(b, HEADS),
        in_specs=[
            pl.BlockSpec((1, N_TILES, HEAD_PAD, TILE), lambda i, h: (i, 0, h, 0)),
            pl.BlockSpec((1, TOK, HEAD_PAD), lambda i, h: (i, 0, h)),
            pl.BlockSpec((1, MLA_V, TOK), lambda i, h: (i, h, 0)),
            pl.BlockSpec((1, N_TILES, MLA_V, TILE), lambda i, h: (i, 0, HEADS + h, 0)),
        ],
        out_specs=pl.BlockSpec((1, out_tiles, MLA_V, TILE), lambda i, h: (i, 0, h, 0)),
        out_shape=jax.ShapeDtypeStruct((b, out_tiles, MLA_WIDTH, TILE), BF16),
        compiler_params=_params("arbitrary", "arbitrary"),
        name="mla_attention",
    )(qml, kml, vml, gt)


def _na_kernel(q_ref, k_ref, v_ref, g_ref, b_ref, o_ref, *, out_tiles):
    row = lax.broadcasted_iota(jnp.int32, (2 * NA_DIM, 1), 0)
    first = row < NA_DIM

    def q_pair(qi):
        qp = q_ref[0, qi]
        zero = jnp.zeros_like(qp)
        return jnp.concatenate([jnp.where(first, qp, zero), jnp.where(first, zero, qp)], axis=1)

    def attend(qi, key_tiles, var):
        q2 = q_pair(qi)
        scores = []
        for n, t in enumerate(key_tiles):
            s = jnp.dot(k_ref[0, t * TILE:(t + 1) * TILE, :], q2, preferred_element_type=F32)
            if t < N_LAT_TILES:
                r0 = (n - 1) * TILE
                s = s + jnp.concatenate([b_ref[var, 0, r0:r0 + TILE, :], b_ref[var, 1, r0:r0 + TILE, :]], axis=1)
            scores.append(s)
        g = g_ref[0, qi]
        halves = []
        for i in range(2):
            o = _online_pv([(s[:, i * TILE:(i + 1) * TILE], _with_ones(v_ref[0, t, i * NA_DIM:(i + 1) * NA_DIM, :]))
                            for s, t in zip(scores, key_tiles)])
            halves.append(_gated(o, g[i * NA_DIM:(i + 1) * NA_DIM]))
        o_ref[0, qi] = jnp.concatenate(halves, axis=0)

    for qi in range(N_LAT_TILES):
        t0 = min(max(qi - 1, 0), N_LAT_TILES - NA_KEY_TILES)
        var = 0 if qi == 0 else (2 if qi == N_LAT_TILES - 1 else 1)
        attend(qi, [N_LAT_TILES] + list(range(t0, t0 + NA_KEY_TILES)), var)
    if out_tiles == N_TILES:
        attend(N_LAT_TILES, [N_LAT_TILES], 0)


def _na_call(qna, kna, vna, gt, bias_t, out_tiles):
    b = qna.shape[0]
    pair = 2 * NA_DIM
    return pl.pallas_call(
        functools.partial(_na_kernel, out_tiles=out_tiles),
        grid=(HEADS // 2, b),
        in_specs=[
            pl.BlockSpec((1, N_TILES, pair, TILE), lambda h, i: (i, 0, h, 0)),
            pl.BlockSpec((1, TOK, pair), lambda h, i: (i, 0, h)),
            pl.BlockSpec((1, N_TILES, pair, TILE), lambda h, i: (i, 0, h, 0)),
            pl.BlockSpec((1, N_TILES, pair, TILE), lambda h, i: (i, 0, h, 0)),
            pl.BlockSpec((3, 2, NA_LOCAL_KEYS, TILE), lambda h, i: (0, h, 0, 0)),
        ],
        out_specs=pl.BlockSpec((1, out_tiles, pair, TILE), lambda h, i: (i, 0, h, 0)),
        out_shape=jax.ShapeDtypeStruct((b, out_tiles, NA_WIDTH, TILE), BF16),
        compiler_params=_params("arbitrary", "arbitrary"),
        name="na_attention",
    )(qna, kna, vna, gt, bias_t)


def _out_kernel(ona_ref, oml_ref, wa_ref, wb_ref, lat_ref, con_ref, mod_ref, fg_ref, o_ref, *, final):
    y = (lax.dot_general(ona_ref[0, 0], wa_ref[...], _TN, preferred_element_type=F32)
         + lax.dot_general(oml_ref[0, 0], wb_ref[...], _TN, preferred_element_type=F32))
    xn = _stream_tile(lat_ref, con_ref) + mod_ref[0, 0, 2:3, :] * y
    if final:
        xn = _rms(xn, fg_ref[...])
    o_ref[0] = xn


def _out_call(ona, oml, wa, wb, lat, con, mod, fg, final):
    b = lat.shape[0]
    tiles = ona.shape[1]
    const = lambda shape: pl.BlockSpec(shape, lambda i, t: (0,) * len(shape))
    return pl.pallas_call(
        functools.partial(_out_kernel, final=final),
        grid=(b, tiles),
        in_specs=[
            pl.BlockSpec((1, 1, NA_WIDTH, TILE), lambda i, t: (i, t, 0, 0)),
            pl.BlockSpec((1, 1, MLA_WIDTH, TILE), lambda i, t: (i, t, 0, 0)),
            const((NA_WIDTH, D_MODEL)),
            const((MLA_WIDTH, D_MODEL)),
        ] + _stream_specs(con) + [
            pl.BlockSpec((1, 1, 3, D_MODEL), lambda i, t: (i, t // N_LAT_TILES, 0, 0)),
            const((1, D_MODEL)),
        ],
        out_specs=pl.BlockSpec((1, TILE, D_MODEL), lambda i, t: (i, t, 0)),
        out_shape=jax.ShapeDtypeStruct((b, tiles * TILE, D_MODEL), F32),
        compiler_params=_params("arbitrary", "arbitrary"),
        name="out_projection",
    )(ona, oml, wa, wb, lat, con, mod, fg)


def _rope_tables():
    t = np.arange(SEQ)
    row = (t // GRID_W).astype(np.float32)
    col = (t % GRID_W).astype(np.float32)
    per_axis = MLA_ROPE // 2
    inv = (1.0 / (ROPE_THETA ** (jnp.arange(0, per_axis, 2, dtype=F32) / per_axis)))
    ang = jnp.concatenate([jnp.asarray(row)[:, None] * inv[None], jnp.asarray(col)[:, None] * inv[None]], axis=-1)
    cos = jnp.concatenate([jnp.cos(ang), jnp.ones((CTX, per_axis), F32)], axis=0)
    sin = jnp.concatenate([jnp.sin(ang), jnp.zeros((CTX, per_axis), F32)], axis=0)
    zeros = jnp.zeros((TOK, HEAD_PAD - 2 * per_axis), F32)
    zh = jnp.zeros((TOK, per_axis), F32)
    ck = jnp.concatenate([cos, cos, zeros], axis=1)
    sa = jnp.concatenate([-sin, zh, zeros], axis=1)
    sb = jnp.concatenate([zh, sin, zeros], axis=1)
    return ck, sa, sb, cos.T, sin.T


def _col_tables():
    k_c = np.arange(GRID_W)[:, None]
    q_c = np.arange(GRID_W)[None, :]
    cs = np.clip(q_c - NA_WIN_W // 2, 0, GRID_W - NA_WIN_W)
    dcol = np.clip(k_c - q_c + NA_WIN_W - 1, 0, 2 * NA_WIN_W - 2).reshape(-1)
    sel = (np.arange(RPB_COLS_PAD)[:, None] == dcol[None, :]).astype(np.float32)
    ok = ((k_c >= cs) & (k_c < cs + NA_WIN_W)).astype(np.float32).reshape(1, -1)
    return jnp.asarray(sel, BF16), jnp.asarray(ok)


def _bias_kernel(rpb_ref, sel_ref, ok_ref, o_ref):
    v = rpb_ref[...] * LOG2E
    hi = v.astype(BF16)
    r1 = v - hi.astype(F32)
    mid = r1.astype(BF16)
    lo = (r1 - mid.astype(F32)).astype(BF16)
    sel = sel_ref[...]
    t = (jnp.dot(hi, sel, preferred_element_type=F32) + jnp.dot(mid, sel, preferred_element_type=F32)
         + jnp.dot(lo, sel, preferred_element_type=F32))
    o_ref[...] = jnp.where(ok_ref[...] > 0.5, t, MASK_VALUE)


def _na_bias(rpb):
    sel, ok = _col_tables()
    lh = rpb.shape[0] * HEADS
    pad = jnp.zeros((lh, RPB_ROWS_PAD, RPB_COLS_PAD), F32)
    pad = pad.at[:, :2 * NA_WIN_H - 1, :2 * NA_WIN_W - 1].set(rpb.reshape(lh, 2 * NA_WIN_H - 1, 2 * NA_WIN_W - 1))
    toe = pl.pallas_call(
        _bias_kernel,
        out_shape=jax.ShapeDtypeStruct((lh * RPB_ROWS_PAD, GRID_W * GRID_W), F32),
        compiler_params=pltpu.CompilerParams(vmem_limit_bytes=VMEM_LIMIT_BYTES),
        name="na_bias_columns",
    )(pad.reshape(lh * RPB_ROWS_PAD, RPB_COLS_PAD), sel, ok)
    toe = toe.reshape(lh, RPB_ROWS_PAD, GRID_W, GRID_W)

    rows = SEQ // GRID_W
    tile_rows = TILE // GRID_W
    key_rows = NA_KEY_TILES * tile_rows
    masked = jnp.full((lh, GRID_W, GRID_W), MASK_VALUE, F32)
    out = []
    for g in (0, 1, N_LAT_TILES - 1):
        t0 = min(max(g - 1, 0), N_LAT_TILES - NA_KEY_TILES)
        blocks = []
        for kr in range(key_rows):
            k_r = t0 * tile_rows + kr
            row_blocks = []
            for qi in range(tile_rows):
                q_r = g * tile_rows + qi
                rs = min(max(q_r - NA_WIN_H // 2, 0), rows - NA_WIN_H)
                inside = rs <= k_r < rs + NA_WIN_H
                row_blocks.append(toe[:, k_r - q_r + NA_WIN_H - 1] if inside else masked)
            blocks.append(jnp.stack(row_blocks, axis=2))
        out.append(jnp.stack(blocks, axis=1).reshape(rpb.shape[0], HEADS, key_rows * GRID_W, TILE))
    return jnp.stack(out, axis=1)


def _layer_weights(w_in, w_uq, w_ukv, w_out):
    cuts = np.cumsum([NA_WIDTH] * 4 + [Q_RANK, KV_RANK, MLA_ROPE, MLA_WIDTH])
    na_q, na_k, na_v, na_g, c_q, c_kv, k_r, ml_g = jnp.split(w_in, [int(c) for c in cuts[:-1]], axis=1)
    half = MLA_ROPE // 2
    kr_blk = jnp.concatenate([k_r[:, 0::2], k_r[:, 1::2], jnp.zeros((D_MODEL, HEAD_PAD - MLA_ROPE), F32)], axis=1)
    wtok = jnp.concatenate([na_k, c_q, c_kv, kr_blk], axis=1).astype(BF16)
    wt = jnp.concatenate([na_q, na_v, na_g, ml_g], axis=1).T.astype(BF16)

    uq = w_uq.reshape(Q_RANK, HEADS, MLA_NOPE + MLA_ROPE)
    uq = jnp.concatenate([uq[..., :MLA_NOPE], uq[..., MLA_NOPE::2], uq[..., MLA_NOPE + 1::2],
                          jnp.zeros((Q_RANK, HEADS, HEAD_PAD - MLA_NOPE - MLA_ROPE), F32)], axis=-1)
    wq = uq.reshape(Q_RANK, HEADS * HEAD_PAD).T.astype(BF16)

    ukv = w_ukv.reshape(KV_RANK, HEADS, MLA_NOPE + MLA_V)
    k_top = jnp.concatenate([ukv[..., :MLA_NOPE], jnp.zeros((KV_RANK, HEADS, HEAD_PAD - MLA_NOPE), F32)], axis=-1)
    eye = jnp.concatenate([jnp.zeros((MLA_ROPE, MLA_NOPE), F32), jnp.eye(MLA_ROPE, dtype=F32),
                           jnp.zeros((MLA_ROPE, HEAD_PAD - MLA_NOPE - MLA_ROPE), F32)], axis=1)
    k_bot = jnp.concatenate([jnp.broadcast_to(eye[:, None], (MLA_ROPE, HEADS, HEAD_PAD)),
                             jnp.zeros((KV_RANK - MLA_ROPE, HEADS, HEAD_PAD), F32)], axis=0)
    wk = jnp.concatenate([k_top, k_bot], axis=0).reshape(2 * KV_RANK, HEADS * HEAD_PAD).astype(BF16)
    wv = ukv[..., MLA_NOPE:].reshape(KV_RANK, MLA_WIDTH).T.astype(BF16)

    wa = w_out[:NA_WIDTH].astype(BF16)
    wb = w_out[NA_WIDTH:].astype(BF16)
    return wtok, wt, wq, wk, wv, wa, wb


def kernel(x, c, ctx, c_ctx, norm_g, w_ada, b_ada, w_in, na_rpb, q_norm_g, w_uq, kv_norm_g, w_ukv, w_out,
           final_norm_g):
    b = x.shape[0]
    tabs = _rope_tables()

    rows = -(-(b + 1) // 8) * 8
    cc = jnp.concatenate([c, c_ctx[None], jnp.zeros((rows - b - 1, D_MODEL), F32)], axis=0)
    ada = _ada_call(cc, w_ada, b_ada)

    bias_t = _na_bias(na_rpb)
    lat, con = x, ctx
    out = None
    for l in range(DEPTH):
        last = l == DEPTH - 1
        mod_lat = ada[l, :b].reshape(b, 1, 3, D_MODEL)
        mod_con = jnp.broadcast_to(ada[l, b].reshape(1, 1, 3, D_MODEL), (b, 1, 3, D_MODEL))
        mod = jnp.concatenate([mod_lat, mod_con], axis=1)
        wtok, wt, wq, wk, wv, wa, wb = _layer_weights(w_in[l], w_uq[l], w_ukv[l], w_out[l])
        kna, qna, vna, gt, qml, kml, vml = _proj_call(
            lat, con, mod, norm_g[l].reshape(1, D_MODEL), wtok, wt, q_norm_g[l].reshape(1, Q_RANK), wq,
            kv_norm_g[l].reshape(1, KV_RANK), wk, wv, tabs)
        out_tiles = N_LAT_TILES if last else N_TILES
        ona = _na_call(qna, kna, vna, gt, bias_t[l], out_tiles)
        oml = _mla_call(qml, kml, vml, gt, out_tiles)
        res = _out_call(ona, oml, wa, wb, lat, con, mod, final_norm_g.reshape(1, D_MODEL), last)
        if last:
            out = res
        else:
            lat = con = res
    return out
```

```python
import functools
import math

import numpy as np
import jax
import jax.numpy as jnp
from jax import lax
from jax.experimental import pallas as pl
from jax.experimental.pallas import tpu as pltpu

D_MODEL = 1024
SEQ = 2048
CTX = 256
TOK = SEQ + CTX
GRID_W = 64
EPS = 1e-6
DEPTH = 2

HEADS = 8
NA_DIM = 64
NA_WIDTH = HEADS * NA_DIM
NA_WIN_H = 8
NA_WIN_W = 16
MLA_NOPE = 64
MLA_ROPE = 32
MLA_V = 64
MLA_WIDTH = HEADS * MLA_V
Q_RANK = 256
KV_RANK = 128
ROPE_THETA = 10000.0
LOG2E = math.log2(math.e)
NA_QSCALE = NA_DIM ** -0.5 * LOG2E
MLA_QSCALE = (MLA_NOPE + MLA_ROPE) ** -0.5 * LOG2E
MASK_VALUE = -1e30

TILE = 256
N_LAT_TILES = SEQ // TILE
N_TILES = TOK // TILE
HEAD_PAD = 128
ONES_ROWS = 16
NA_KEY_TILES = 3
NA_LOCAL_KEYS = NA_KEY_TILES * TILE
KEY_CHUNK = 256
FINAL_TILES = 4
RPB_ROWS_PAD = 16
RPB_COLS_PAD = 32

VMEM_LIMIT_BYTES = 48 * 1024 * 1024

BF16 = jnp.bfloat16
F32 = jnp.float32

_NT = (((1,), (1,)), ((), ()))
_TN = (((0,), (0,)), ((), ()))


def _params(*semantics):
    return pltpu.CompilerParams(dimension_semantics=semantics, vmem_limit_bytes=VMEM_LIMIT_BYTES)


def _rms(v, g):
    return v * lax.rsqrt(jnp.mean(v * v, axis=-1, keepdims=True) + EPS) * g


def _ada_kernel(c_ref, w_ref, b_ref, o_ref):
    cv = c_ref[...]
    a = (cv * jax.nn.sigmoid(cv)).astype(BF16)
    o_ref[0] = jnp.dot(a, w_ref[0].astype(BF16), preferred_element_type=F32) + b_ref[0]


def _ada_call(cc, w_ada, b_ada):
    rows = cc.shape[0]
    return pl.pallas_call(
        _ada_kernel,
        grid=(DEPTH, 3),
        in_specs=[
            pl.BlockSpec((rows, D_MODEL), lambda l, j: (0, 0)),
            pl.BlockSpec((1, D_MODEL, D_MODEL), lambda l, j: (l, 0, j)),
            pl.BlockSpec((1, 1, D_MODEL), lambda l, j: (l, 0, j)),
        ],
        out_specs=pl.BlockSpec((1, rows, D_MODEL), lambda l, j: (l, 0, j)),
        out_shape=jax.ShapeDtypeStruct((DEPTH, rows, 3 * D_MODEL), F32),
        compiler_params=_params("arbitrary", "arbitrary"),
        name="ada_modulation",
    )(cc, w_ada, b_ada.reshape(DEPTH, 1, 3 * D_MODEL))


def _stream_specs(con):
    con_block = con.shape[1] // CTX - 1
    return [pl.BlockSpec((1, TILE, D_MODEL), lambda i, t: (i, jnp.minimum(t, N_LAT_TILES - 1), 0)),
            pl.BlockSpec((1, CTX, D_MODEL), lambda i, t: (i, con_block, 0))]


def _stream_tile(lat_ref, con_ref):
    return jnp.where(pl.program_id(1) == N_LAT_TILES, con_ref[0], lat_ref[0])


def _project(x, mod_ref, ng_ref, wtok_ref, wt_ref, qg_ref, wq_ref, kvg_ref, wk_ref, wv_ref,
             ck_ref, sa_ref, sb_ref, cos_ref, sin_ref,
             kna_ref, qna_ref, vna_ref, gt_ref, qml_ref, kml_ref, vml_ref):
    shift = mod_ref[0, 0, 0:1, :]
    scale = mod_ref[0, 0, 1:2, :]
    h = _rms(x, ng_ref[...]) * (1.0 + scale) + shift
    hb = h.astype(BF16)
    r_tok = jnp.dot(hb, wtok_ref[...], preferred_element_type=F32)
    r_t = lax.dot_general(wt_ref[...], hb, _NT, preferred_element_type=F32)

    kna_ref[0] = r_tok[:, 0:NA_WIDTH].astype(BF16)
    qna_ref[0, 0] = (r_t[0:NA_WIDTH] * NA_QSCALE).astype(BF16)
    vna_ref[0, 0] = r_t[NA_WIDTH:2 * NA_WIDTH].astype(BF16)
    gates = r_t[2 * NA_WIDTH:]
    gt_ref[0, 0] = (gates * jax.nn.sigmoid(gates)).astype(BF16)

    c0 = NA_WIDTH
    cq = _rms(r_tok[:, c0:c0 + Q_RANK], qg_ref[...]).astype(BF16)
    ckv = _rms(r_tok[:, c0 + Q_RANK:c0 + Q_RANK + KV_RANK], kvg_ref[...]).astype(BF16)
    kr = r_tok[:, c0 + Q_RANK + KV_RANK:]

    q_t = lax.dot_general(wq_ref[...], cq, _NT, preferred_element_type=F32) * MLA_QSCALE
    cos_t = cos_ref[...]
    sin_t = sin_ref[...]
    half = MLA_ROPE // 2
    for hd in range(HEADS):
        r0 = hd * HEAD_PAD
        x1 = q_t[r0 + MLA_NOPE:r0 + MLA_NOPE + half]
        x2 = q_t[r0 + MLA_NOPE + half:r0 + MLA_NOPE + MLA_ROPE]
        qml_ref[0, 0, r0:r0 + MLA_NOPE, :] = q_t[r0:r0 + MLA_NOPE].astype(BF16)
        qml_ref[0, 0, r0 + MLA_NOPE:r0 + MLA_NOPE + half, :] = (x1 * cos_t - x2 * sin_t).astype(BF16)
        qml_ref[0, 0, r0 + MLA_NOPE + half:r0 + MLA_NOPE + MLA_ROPE, :] = (x1 * sin_t + x2 * cos_t).astype(BF16)
        qml_ref[0, 0, r0 + MLA_NOPE + MLA_ROPE:r0 + HEAD_PAD, :] = q_t[r0 + MLA_NOPE + MLA_ROPE:r0 + HEAD_PAD].astype(BF16)

    krr = (kr * ck_ref[...] + pltpu.roll(kr, HEAD_PAD - half, 1) * sa_ref[...]
           + pltpu.roll(kr, half, 1) * sb_ref[...]).astype(BF16)
    lhs = jnp.concatenate([ckv, krr], axis=1)
    kml_ref[0] = jnp.dot(lhs, wk_ref[...], preferred_element_type=F32).astype(BF16)
    vml_ref[0] = lax.dot_general(wv_ref[...], ckv, _NT, preferred_element_type=F32).astype(BF16)


def _proj_kernel(lat_ref, con_ref, *refs):
    _project(_stream_tile(lat_ref, con_ref), *refs)


def _const_spec(shape):
    return pl.BlockSpec(shape, lambda i, t: (0,) * len(shape))


def _mod_spec():
    return pl.BlockSpec((1, 1, 3, D_MODEL), lambda i, t: (i, t // N_LAT_TILES, 0, 0))


def _proj_specs(b):
    const = _const_spec
    tile4 = lambda rows: pl.BlockSpec((1, 1, rows, TILE), lambda i, t: (i, t, 0, 0))
    in_specs = [
        _mod_spec(),
        const((1, D_MODEL)),
        const((D_MODEL, 1024)),
        const((2048, D_MODEL)),
        const((1, Q_RANK)),
        const((HEADS * HEAD_PAD, Q_RANK)),
        const((1, KV_RANK)),
        const((2 * KV_RANK, HEADS * HEAD_PAD)),
        const((MLA_WIDTH, KV_RANK)),
        pl.BlockSpec((TILE, HEAD_PAD), lambda i, t: (t, 0)),
        pl.BlockSpec((TILE, HEAD_PAD), lambda i, t: (t, 0)),
        pl.BlockSpec((TILE, HEAD_PAD), lambda i, t: (t, 0)),
        pl.BlockSpec((MLA_ROPE // 2, TILE), lambda i, t: (0, t)),
        pl.BlockSpec((MLA_ROPE // 2, TILE), lambda i, t: (0, t)),
    ]
    out_specs = [
        pl.BlockSpec((1, TILE, NA_WIDTH), lambda i, t: (i, t, 0)),
        tile4(NA_WIDTH),
        tile4(NA_WIDTH),
        tile4(NA_WIDTH + MLA_WIDTH),
        tile4(HEADS * HEAD_PAD),
        pl.BlockSpec((1, TILE, HEADS * HEAD_PAD), lambda i, t: (i, t, 0)),
        pl.BlockSpec((1, MLA_WIDTH, TILE), lambda i, t: (i, 0, t)),
    ]
    out_shape = [
        jax.ShapeDtypeStruct((b, TOK, NA_WIDTH), BF16),
        jax.ShapeDtypeStruct((b, N_TILES, NA_WIDTH, TILE), BF16),
        jax.ShapeDtypeStruct((b, N_TILES, NA_WIDTH, TILE), BF16),
        jax.ShapeDtypeStruct((b, N_TILES, NA_WIDTH + MLA_WIDTH, TILE), BF16),
        jax.ShapeDtypeStruct((b, N_TILES, HEADS * HEAD_PAD, TILE), BF16),
        jax.ShapeDtypeStruct((b, TOK, HEADS * HEAD_PAD), BF16),
        jax.ShapeDtypeStruct((b, MLA_WIDTH, TOK), BF16),
    ]
    return in_specs, out_specs, out_shape


def _proj_call(lat, con, proj_args):
    b = lat.shape[0]
    in_specs, out_specs, out_shape = _proj_specs(b)
    return pl.pallas_call(
        _proj_kernel,
        grid=(b, N_TILES),
        in_specs=_stream_specs(con) + in_specs,
        out_specs=out_specs,
        out_shape=out_shape,
        compiler_params=_params("arbitrary", "arbitrary"),
        name="norm_projections",
    )(lat, con, *proj_args)


def _with_ones(vt):
    return jnp.concatenate([vt, jnp.ones((ONES_ROWS, vt.shape[1]), vt.dtype)], axis=0)


def _online_pv(chunks):
    m = o = None
    for s, v in chunks:
        s = s.astype(BF16)
        cm = jnp.max(s, axis=0, keepdims=True)
        if m is None:
            m = cm
            o = jnp.dot(v, jnp.exp2(s - m), preferred_element_type=F32)
        else:
            m_new = jnp.maximum(m, cm)
            alpha = jnp.exp2(m.astype(F32) - m_new.astype(F32))
            o = o * alpha + jnp.dot(v, jnp.exp2(s - m_new), preferred_element_type=F32)
            m = m_new
    return o


def _gated(o, gate):
    dv = gate.shape[0]
    return (o[:dv] * (1.0 / o[dv:dv + 1]) * gate.astype(F32)).astype(BF16)


def _mla_kernel(q_ref, k_ref, v_ref, g_ref, o_ref, *, out_tiles):
    v_aug = _with_ones(v_ref[0])
    chunks = [(a, min(a + KEY_CHUNK, TOK)) for a in range(0, TOK, KEY_CHUNK)]

    for j in range(N_LAT_TILES // 2):
        q2 = jnp.concatenate([q_ref[0, 2 * j], q_ref[0, 2 * j + 1]], axis=1)
        o = _online_pv([(jnp.dot(k_ref[0, a:b, :], q2, preferred_element_type=F32), v_aug[:, a:b])
                        for a, b in chunks])
        for i in range(2):
            o_ref[0, 2 * j + i] = _gated(o[:, i * TILE:(i + 1) * TILE], g_ref[0, 2 * j + i])

    if out_tiles == N_TILES:
        qi = N_LAT_TILES
        s = jnp.dot(k_ref[0, SEQ:, :], q_ref[0, qi], preferred_element_type=F32)
        o_ref[0, qi] = _gated(_online_pv([(s, v_aug[:, SEQ:])]), g_ref[0, qi])


def _mla_call(qml, kml, vml, gt, out_tiles):
    b = qml.shape[0]
    return pl.pallas_call(
        functools.partial(_mla_kernel, out_tiles=out_tiles),
        grid=(b, HEADS),
        in_specs=[
            pl.BlockSpec((1, N_TILES, HEAD_PAD, TILE), lambda i, h: (i, 0, h, 0)),
            pl.BlockSpec((1, TOK, HEAD_PAD), lambda i, h: (i, 0, h)),
            pl.BlockSpec((1, MLA_V, TOK), lambda i, h: (i, h, 0)),
            pl.BlockSpec((1, N_TILES, MLA_V, TILE), lambda i, h: (i, 0, HEADS + h, 0)),
        ],
        out_specs=pl.BlockSpec((1, out_tiles, MLA_V, TILE), lambda i, h: (i, 0, h, 0)),
        out_shape=jax.ShapeDtypeStruct((b, out_tiles, MLA_WIDTH, TILE), BF16),
        compiler_params=_params("arbitrary", "arbitrary"),
        name="mla_attention",
    )(qml, kml, vml, gt)


def _na_window(qi):
    return min(max(qi - 1, 0), N_LAT_TILES - NA_KEY_TILES)


def _na_key_tiles(qi):
    rows = SEQ // GRID_W
    tile_rows = TILE // GRID_W
    lo = min(max(qi * tile_rows - NA_WIN_H // 2, 0), rows - NA_WIN_H)
    hi = min(max((qi + 1) * tile_rows - 1 - NA_WIN_H // 2, 0), rows - NA_WIN_H) + NA_WIN_H - 1
    return list(range(lo // tile_rows, hi // tile_rows + 1))


def _na_kernel(q_ref, k_ref, v_ref, g_ref, b_ref, o_ref, *, out_tiles):
    row = lax.broadcasted_iota(jnp.int32, (2 * NA_DIM, 1), 0)
    first = row < NA_DIM

    def q_pair(qi):
        qp = q_ref[0, qi]
        zero = jnp.zeros_like(qp)
        return jnp.concatenate([jnp.where(first, qp, zero), jnp.where(first, zero, qp)], axis=1)

    def attend(qi, key_tiles, var):
        q2 = q_pair(qi)
        scores = []
        for t in key_tiles:
            s = jnp.dot(k_ref[0, t * TILE:(t + 1) * TILE, :], q2, preferred_element_type=F32)
            if t < N_LAT_TILES:
                r0 = (t - _na_window(qi)) * TILE
                s = s + jnp.concatenate([b_ref[var, 0, r0:r0 + TILE, :], b_ref[var, 1, r0:r0 + TILE, :]], axis=1)
            scores.append(s)
        g = g_ref[0, qi]
        halves = []
        for i in range(2):
            o = _online_pv([(s[:, i * TILE:(i + 1) * TILE], _with_ones(v_ref[0, t, i * NA_DIM:(i + 1) * NA_DIM, :]))
                            for s, t in zip(scores, key_tiles)])
            halves.append(_gated(o, g[i * NA_DIM:(i + 1) * NA_DIM]))
        o_ref[0, qi] = jnp.concatenate(halves, axis=0)

    for qi in range(N_LAT_TILES):
        var = 0 if qi == 0 else (2 if qi == N_LAT_TILES - 1 else 1)
        attend(qi, [N_LAT_TILES] + _na_key_tiles(qi), var)
    if out_tiles == N_TILES:
        attend(N_LAT_TILES, [N_LAT_TILES], 0)


def _na_call(qna, kna, vna, gt, bias_t, out_tiles):
    b = qna.shape[0]
    pair = 2 * NA_DIM
    return pl.pallas_call(
        functools.partial(_na_kernel, out_tiles=out_tiles),
        grid=(HEADS // 2, b),
        in_specs=[
            pl.BlockSpec((1, N_TILES, pair, TILE), lambda h, i: (i, 0, h, 0)),
            pl.BlockSpec((1, TOK, pair), lambda h, i: (i, 0, h)),
            pl.BlockSpec((1, N_TILES, pair, TILE), lambda h, i: (i, 0, h, 0)),
            pl.BlockSpec((1, N_TILES, pair, TILE), lambda h, i: (i, 0, h, 0)),
            pl.BlockSpec((3, 2, NA_LOCAL_KEYS, TILE), lambda h, i: (0, h, 0, 0)),
        ],
        out_specs=pl.BlockSpec((1, out_tiles, pair, TILE), lambda h, i: (i, 0, h, 0)),
        out_shape=jax.ShapeDtypeStruct((b, out_tiles, NA_WIDTH, TILE), BF16),
        compiler_params=_params("arbitrary", "arbitrary"),
        name="na_attention",
    )(qna, kna, vna, gt, bias_t)


def _residual(ona, oml, wa_ref, wb_ref, x, gate):
    y = (lax.dot_general(ona, wa_ref[...], _TN, preferred_element_type=F32)
         + lax.dot_general(oml, wb_ref[...], _TN, preferred_element_type=F32))
    return x + gate * y


def _out_proj_kernel(ona_ref, oml_ref, wa_ref, wb_ref, lat_ref, con_ref, gmod_ref, *refs):
    x_ref = refs[-8]
    xn = _residual(ona_ref[0, 0], oml_ref[0, 0], wa_ref, wb_ref, _stream_tile(lat_ref, con_ref),
                   gmod_ref[0, 0, 2:3, :])
    x_ref[0] = xn
    _project(xn, *refs[:-8], *refs[-7:])


def _out_proj_call(ona, oml, wa, wb, lat, con, mod, proj_args):
    b = lat.shape[0]
    in_specs, out_specs, out_shape = _proj_specs(b)
    att = lambda width: pl.BlockSpec((1, 1, width, TILE), lambda i, t: (i, t, 0, 0))
    return pl.pallas_call(
        _out_proj_kernel,
        grid=(b, N_TILES),
        in_specs=[att(NA_WIDTH), att(MLA_WIDTH), _const_spec((NA_WIDTH, D_MODEL)), _const_spec((MLA_WIDTH, D_MODEL))]
        + _stream_specs(con) + [_mod_spec()] + in_specs,
        out_specs=[pl.BlockSpec((1, TILE, D_MODEL), lambda i, t: (i, t, 0))] + out_specs,
        out_shape=[jax.ShapeDtypeStruct((b, TOK, D_MODEL), F32)] + out_shape,
        compiler_params=_params("arbitrary", "arbitrary"),
        name="out_projection_next_projections",
    )(ona, oml, wa, wb, lat, con, mod, *proj_args)


def _final_kernel(ona_ref, oml_ref, wa_ref, wb_ref, x_ref, mod_ref, fg_ref, o_ref):
    gate = mod_ref[0, 0, 2:3, :]
    for u in range(FINAL_TILES):
        rows = slice(u * TILE, (u + 1) * TILE)
        xn = _residual(ona_ref[0, u], oml_ref[0, u], wa_ref, wb_ref, x_ref[0, rows, :], gate)
        o_ref[0, rows, :] = _rms(xn, fg_ref[...])


def _final_call(ona, oml, wa, wb, stream, mod, fg):
    b = stream.shape[0]
    att = lambda width: pl.BlockSpec((1, FINAL_TILES, width, TILE), lambda i, t: (i, t, 0, 0))
    rows = FINAL_TILES * TILE
    return pl.pallas_call(
        _final_kernel,
        grid=(b, N_LAT_TILES // FINAL_TILES),
        in_specs=[att(NA_WIDTH), att(MLA_WIDTH), _const_spec((NA_WIDTH, D_MODEL)), _const_spec((MLA_WIDTH, D_MODEL)),
                  pl.BlockSpec((1, rows, D_MODEL), lambda i, t: (i, t, 0)),
                  pl.BlockSpec((1, 1, 3, D_MODEL), lambda i, t: (i, 0, 0, 0)),
                  _const_spec((1, D_MODEL))],
        out_specs=pl.BlockSpec((1, rows, D_MODEL), lambda i, t: (i, t, 0)),
        out_shape=jax.ShapeDtypeStruct((b, SEQ, D_MODEL), F32),
        compiler_params=_params("arbitrary", "arbitrary"),
        name="out_projection_final_norm",
    )(ona, oml, wa, wb, stream, mod, fg)


def _rope_tables():
    t = np.arange(SEQ)
    row = (t // GRID_W).astype(np.float32)
    col = (t % GRID_W).astype(np.float32)
    per_axis = MLA_ROPE // 2
    inv = (1.0 / (ROPE_THETA ** (jnp.arange(0, per_axis, 2, dtype=F32) / per_axis)))
    ang = jnp.concatenate([jnp.asarray(row)[:, None] * inv[None], jnp.asarray(col)[:, None] * inv[None]], axis=-1)
    cos = jnp.concatenate([jnp.cos(ang), jnp.ones((CTX, per_axis), F32)], axis=0)
    sin = jnp.concatenate([jnp.sin(ang), jnp.zeros((CTX, per_axis), F32)], axis=0)
    zeros = jnp.zeros((TOK, HEAD_PAD - 2 * per_axis), F32)
    zh = jnp.zeros((TOK, per_axis), F32)
    ck = jnp.concatenate([cos, cos, zeros], axis=1)
    sa = jnp.concatenate([-sin, zh, zeros], axis=1)
    sb = jnp.concatenate([zh, sin, zeros], axis=1)
    return ck, sa, sb, cos.T, sin.T


def _col_tables():
    k_c = np.arange(GRID_W)[:, None]
    q_c = np.arange(GRID_W)[None, :]
    cs = np.clip(q_c - NA_WIN_W // 2, 0, GRID_W - NA_WIN_W)
    dcol = np.clip(k_c - q_c + NA_WIN_W - 1, 0, 2 * NA_WIN_W - 2).reshape(-1)
    sel = (np.arange(RPB_COLS_PAD)[:, None] == dcol[None, :]).astype(np.float32)
    ok = ((k_c >= cs) & (k_c < cs + NA_WIN_W)).astype(np.float32).reshape(1, -1)
    return jnp.asarray(sel, BF16), jnp.asarray(ok)


def _bias_kernel(rpb_ref, sel_ref, ok_ref, o_ref):
    v = rpb_ref[...] * LOG2E
    hi = v.astype(BF16)
    r1 = v - hi.astype(F32)
    mid = r1.astype(BF16)
    lo = (r1 - mid.astype(F32)).astype(BF16)
    sel = sel_ref[...]
    t = (jnp.dot(hi, sel, preferred_element_type=F32) + jnp.dot(mid, sel, preferred_element_type=F32)
         + jnp.dot(lo, sel, preferred_element_type=F32))
    o_ref[...] = jnp.where(ok_ref[...] > 0.5, t, MASK_VALUE)


def _bias_rows_kernel(toe_ref, o_ref):
    rows = SEQ // GRID_W
    tile_rows = TILE // GRID_W
    masked = jnp.full((GRID_W, GRID_W), MASK_VALUE, F32)
    for v, g in enumerate((0, 1, N_LAT_TILES - 1)):
        for kr in range(NA_KEY_TILES * tile_rows):
            k_r = _na_window(g) * tile_rows + kr
            blocks = []
            for qi in range(tile_rows):
                q_r = g * tile_rows + qi
                rs = min(max(q_r - NA_WIN_H // 2, 0), rows - NA_WIN_H)
                inside = rs <= k_r < rs + NA_WIN_H
                blocks.append(toe_ref[0, k_r - q_r + NA_WIN_H - 1] if inside else masked)
            o_ref[0, v, 0, kr * GRID_W:(kr + 1) * GRID_W, :] = jnp.concatenate(blocks, axis=1)


def _na_bias(rpb):
    sel, ok = _col_tables()
    lh = rpb.shape[0] * HEADS
    pad = jnp.zeros((lh, RPB_ROWS_PAD, RPB_COLS_PAD), F32)
    pad = pad.at[:, :2 * NA_WIN_H - 1, :2 * NA_WIN_W - 1].set(rpb.reshape(lh, 2 * NA_WIN_H - 1, 2 * NA_WIN_W - 1))
    toe = pl.pallas_call(
        _bias_kernel,
        out_shape=jax.ShapeDtypeStruct((lh * RPB_ROWS_PAD, GRID_W * GRID_W), F32),
        compiler_params=pltpu.CompilerParams(vmem_limit_bytes=VMEM_LIMIT_BYTES),
        name="na_bias_columns",
    )(pad.reshape(lh * RPB_ROWS_PAD, RPB_COLS_PAD), sel, ok)
    toe = toe.reshape(lh, RPB_ROWS_PAD, GRID_W, GRID_W)
    return pl.pallas_call(
        _bias_rows_kernel,
        grid=(lh,),
        in_specs=[pl.BlockSpec((1, RPB_ROWS_PAD, GRID_W, GRID_W), lambda n: (n, 0, 0, 0))],
        out_specs=pl.BlockSpec((1, 3, 1, NA_LOCAL_KEYS, TILE), lambda n: (n // HEADS, 0, n % HEADS, 0, 0)),
        out_shape=jax.ShapeDtypeStruct((rpb.shape[0], 3, HEADS, NA_LOCAL_KEYS, TILE), F32),
        compiler_params=_params("arbitrary"),
        name="na_bias_rows",
    )(toe)


def _layer_weights(w_in, w_uq, w_ukv, w_out):
    cuts = np.cumsum([NA_WIDTH] * 4 + [Q_RANK, KV_RANK, MLA_ROPE, MLA_WIDTH])
    na_q, na_k, na_v, na_g, c_q, c_kv, k_r, ml_g = jnp.split(w_in, [int(c) for c in cuts[:-1]], axis=1)
    kr_blk = jnp.concatenate([k_r[:, 0::2], k_r[:, 1::2], jnp.zeros((D_MODEL, HEAD_PAD - MLA_ROPE), F32)], axis=1)
    wtok = jnp.concatenate([na_k, c_q, c_kv, kr_blk], axis=1).astype(BF16)
    wt = jnp.concatenate([na_q, na_v, na_g, ml_g], axis=1).T.astype(BF16)

    uq = w_uq.reshape(Q_RANK, HEADS, MLA_NOPE + MLA_ROPE)
    uq = jnp.concatenate([uq[..., :MLA_NOPE], uq[..., MLA_NOPE::2], uq[..., MLA_NOPE + 1::2],
                          jnp.zeros((Q_RANK, HEADS, HEAD_PAD - MLA_NOPE - MLA_ROPE), F32)], axis=-1)
    wq = uq.reshape(Q_RANK, HEADS * HEAD_PAD).T.astype(BF16)

    ukv = w_ukv.reshape(KV_RANK, HEADS, MLA_NOPE + MLA_V)
    k_top = jnp.concatenate([ukv[..., :MLA_NOPE], jnp.zeros((KV_RANK, HEADS, HEAD_PAD - MLA_NOPE), F32)], axis=-1)
    eye = jnp.concatenate([jnp.zeros((MLA_ROPE, MLA_NOPE), F32), jnp.eye(MLA_ROPE, dtype=F32),
                           jnp.zeros((MLA_ROPE, HEAD_PAD - MLA_NOPE - MLA_ROPE), F32)], axis=1)
    k_bot = jnp.concatenate([jnp.broadcast_to(eye[:, None], (MLA_ROPE, HEADS, HEAD_PAD)),
                             jnp.zeros((KV_RANK - MLA_ROPE, HEADS, HEAD_PAD), F32)], axis=0)
    wk = jnp.concatenate([k_top, k_bot], axis=0).reshape(2 * KV_RANK, HEADS * HEAD_PAD).astype(BF16)
    wv = ukv[..., MLA_NOPE:].reshape(KV_RANK, MLA_WIDTH).T.astype(BF16)

    wa = w_out[:NA_WIDTH].astype(BF16)
    wb = w_out[NA_WIDTH:].astype(BF16)
    return wtok, wt, wq, wk, wv, wa, wb


def kernel(x, c, ctx, c_ctx, norm_g, w_ada, b_ada, w_in, na_rpb, q_norm_g, w_uq, kv_norm_g, w_ukv, w_out,
           final_norm_g):
    b = x.shape[0]
    tabs = _rope_tables()

    rows = -(-(b + 1) // 8) * 8
    cc = jnp.concatenate([c, c_ctx[None], jnp.zeros((rows - b - 1, D_MODEL), F32)], axis=0)
    ada = _ada_call(cc, w_ada, b_ada)
    bias_t = _na_bias(na_rpb)

    def layer_inputs(l):
        mod_lat = ada[l, :b].reshape(b, 1, 3, D_MODEL)
        mod_con = jnp.broadcast_to(ada[l, b].reshape(1, 1, 3, D_MODEL), (b, 1, 3, D_MODEL))
        mod = jnp.concatenate([mod_lat, mod_con], axis=1)
        wtok, wt, wq, wk, wv, wa, wb = _layer_weights(w_in[l], w_uq[l], w_ukv[l], w_out[l])
        proj_args = (mod, norm_g[l].reshape(1, D_MODEL), wtok, wt, q_norm_g[l].reshape(1, Q_RANK), wq,
                     kv_norm_g[l].reshape(1, KV_RANK), wk, wv, *tabs)
        return mod, proj_args, wa, wb

    lat, con = x, ctx
    mod, proj_args, wa, wb = layer_inputs(0)
    proj = _proj_call(lat, con, proj_args)
    for l in range(DEPTH):
        last = l == DEPTH - 1
        kna, qna, vna, gt, qml, kml, vml = proj
        out_tiles = N_LAT_TILES if last else N_TILES
        ona = _na_call(qna, kna, vna, gt, bias_t[l], out_tiles)
        oml = _mla_call(qml, kml, vml, gt, out_tiles)
        if last:
            return _final_call(ona, oml, wa, wb, lat, mod, final_norm_g.reshape(1, D_MODEL))
        nxt_mod, nxt_args, nxt_wa, nxt_wb = layer_inputs(l + 1)
        stream, *proj = _out_proj_call(ona, oml, wa, wb, lat, con, mod, nxt_args)
        lat = con = stream
        mod, wa, wb = nxt_mod, nxt_wa, nxt_wb
```

```python
import functools
import math

import numpy as np
import jax
import jax.numpy as jnp
from jax import lax
from jax.experimental import pallas as pl
from jax.experimental.pallas import tpu as pltpu

D_MODEL = 1024
SEQ = 2048
CTX = 256
TOK = SEQ + CTX
GRID_W = 64
EPS = 1e-6
DEPTH = 2

HEADS = 8
NA_DIM = 64
NA_WIDTH = HEADS * NA_DIM
NA_WIN_H = 8
NA_WIN_W = 16
MLA_NOPE = 64
MLA_ROPE = 32
MLA_V = 64
MLA_WIDTH = HEADS * MLA_V
Q_RANK = 256
KV_RANK = 128
ROPE_THETA = 10000.0
LOG2E = math.log2(math.e)
NA_QSCALE = NA_DIM ** -0.5 * LOG2E
MLA_QSCALE = (MLA_NOPE + MLA_ROPE) ** -0.5 * LOG2E
MASK_VALUE = -1e30

TILE = 256
N_LAT_TILES = SEQ // TILE
N_TILES = TOK // TILE
HEAD_PAD = 128
ONES_ROWS = 16
NA_KEY_TILES = 3
NA_LOCAL_KEYS = NA_KEY_TILES * TILE
KEY_CHUNK = 256
FINAL_TILES = 4
MLA_HEADS_PER_STEP = 4
NA_PAIRS_PER_STEP = 2
RPB_ROWS_PAD = 16
RPB_COLS_PAD = 32

VMEM_LIMIT_BYTES = 48 * 1024 * 1024

BF16 = jnp.bfloat16
F32 = jnp.float32

_NT = (((1,), (1,)), ((), ()))
_TN = (((0,), (0,)), ((), ()))


def _params(*semantics):
    return pltpu.CompilerParams(dimension_semantics=semantics, vmem_limit_bytes=VMEM_LIMIT_BYTES)


def _rms(v, g):
    return v * lax.rsqrt(jnp.mean(v * v, axis=-1, keepdims=True) + EPS) * g


def _ada_kernel(c_ref, w_ref, b_ref, o_ref):
    cv = c_ref[...]
    a = (cv * jax.nn.sigmoid(cv)).astype(BF16)
    o_ref[0] = jnp.dot(a, w_ref[0].astype(BF16), preferred_element_type=F32) + b_ref[0]


def _ada_call(cc, w_ada, b_ada):
    rows = cc.shape[0]
    return pl.pallas_call(
        _ada_kernel,
        grid=(DEPTH, 3),
        in_specs=[
            pl.BlockSpec((rows, D_MODEL), lambda l, j: (0, 0)),
            pl.BlockSpec((1, D_MODEL, D_MODEL), lambda l, j: (l, 0, j)),
            pl.BlockSpec((1, 1, D_MODEL), lambda l, j: (l, 0, j)),
        ],
        out_specs=pl.BlockSpec((1, rows, D_MODEL), lambda l, j: (l, 0, j)),
        out_shape=jax.ShapeDtypeStruct((DEPTH, rows, 3 * D_MODEL), F32),
        compiler_params=_params("arbitrary", "arbitrary"),
        name="ada_modulation",
    )(cc, w_ada, b_ada.reshape(DEPTH, 1, 3 * D_MODEL))


def _stream_specs(con):
    con_block = con.shape[1] // CTX - 1
    return [pl.BlockSpec((1, TILE, D_MODEL), lambda i, t: (i, jnp.minimum(t, N_LAT_TILES - 1), 0)),
            pl.BlockSpec((1, CTX, D_MODEL), lambda i, t: (i, con_block, 0))]


def _stream_tile(lat_ref, con_ref):
    return jnp.where(pl.program_id(1) == N_LAT_TILES, con_ref[0], lat_ref[0])


def _project(x, mod_ref, ng_ref, wtok_ref, wt_ref, qg_ref, wq_ref, kvg_ref, wk_ref, wv_ref,
             ck_ref, sa_ref, sb_ref, cos_ref, sin_ref,
             kna_ref, qna_ref, vna_ref, gt_ref, qml_ref, kml_ref, vml_ref):
    shift = mod_ref[0, 0, 0:1, :]
    scale = mod_ref[0, 0, 1:2, :]
    h = _rms(x, ng_ref[...]) * (1.0 + scale) + shift
    hb = h.astype(BF16)
    r_tok = jnp.dot(hb, wtok_ref[...], preferred_element_type=F32)
    r_t = lax.dot_general(wt_ref[...], hb, _NT, preferred_element_type=F32)

    kna_ref[0] = r_tok[:, 0:NA_WIDTH].astype(BF16)
    qna_ref[0, 0] = (r_t[0:NA_WIDTH] * NA_QSCALE).astype(BF16)
    vna_ref[0, 0] = r_t[NA_WIDTH:2 * NA_WIDTH].astype(BF16)
    gates = r_t[2 * NA_WIDTH:]
    gt_ref[0, 0] = (gates * jax.nn.sigmoid(gates)).astype(BF16)

    c0 = NA_WIDTH
    cq = _rms(r_tok[:, c0:c0 + Q_RANK], qg_ref[...]).astype(BF16)
    ckv = _rms(r_tok[:, c0 + Q_RANK:c0 + Q_RANK + KV_RANK], kvg_ref[...]).astype(BF16)
    kr = r_tok[:, c0 + Q_RANK + KV_RANK:]

    q_t = lax.dot_general(wq_ref[...], cq, _NT, preferred_element_type=F32) * MLA_QSCALE
    cos_t = cos_ref[...]
    sin_t = sin_ref[...]
    half = MLA_ROPE // 2
    for hd in range(HEADS):
        r0 = hd * HEAD_PAD
        x1 = q_t[r0 + MLA_NOPE:r0 + MLA_NOPE + half]
        x2 = q_t[r0 + MLA_NOPE + half:r0 + MLA_NOPE + MLA_ROPE]
        qml_ref[0, 0, r0:r0 + MLA_NOPE, :] = q_t[r0:r0 + MLA_NOPE].astype(BF16)
        qml_ref[0, 0, r0 + MLA_NOPE:r0 + MLA_NOPE + half, :] = (x1 * cos_t - x2 * sin_t).astype(BF16)
        qml_ref[0, 0, r0 + MLA_NOPE + half:r0 + MLA_NOPE + MLA_ROPE, :] = (x1 * sin_t + x2 * cos_t).astype(BF16)
        qml_ref[0, 0, r0 + MLA_NOPE + MLA_ROPE:r0 + HEAD_PAD, :] = q_t[r0 + MLA_NOPE + MLA_ROPE:r0 + HEAD_PAD].astype(BF16)

    krr = (kr * ck_ref[...] + pltpu.roll(kr, HEAD_PAD - half, 1) * sa_ref[...]
           + pltpu.roll(kr, half, 1) * sb_ref[...]).astype(BF16)
    lhs = jnp.concatenate([ckv, krr], axis=1)
    kml_ref[0] = jnp.dot(lhs, wk_ref[...], preferred_element_type=F32).astype(BF16)
    vml_ref[0] = lax.dot_general(wv_ref[...], ckv, _NT, preferred_element_type=F32).astype(BF16)


def _proj_kernel(lat_ref, con_ref, *refs):
    _project(_stream_tile(lat_ref, con_ref), *refs)


def _const_spec(shape):
    return pl.BlockSpec(shape, lambda i, t: (0,) * len(shape))


def _mod_spec():
    return pl.BlockSpec((1, 1, 3, D_MODEL), lambda i, t: (i, t // N_LAT_TILES, 0, 0))


def _proj_specs(b):
    const = _const_spec
    tile4 = lambda rows: pl.BlockSpec((1, 1, rows, TILE), lambda i, t: (i, t, 0, 0))
    in_specs = [
        _mod_spec(),
        const((1, D_MODEL)),
        const((D_MODEL, 1024)),
        const((2048, D_MODEL)),
        const((1, Q_RANK)),
        const((HEADS * HEAD_PAD, Q_RANK)),
        const((1, KV_RANK)),
        const((2 * KV_RANK, HEADS * HEAD_PAD)),
        const((MLA_WIDTH, KV_RANK)),
        pl.BlockSpec((TILE, HEAD_PAD), lambda i, t: (t, 0)),
        pl.BlockSpec((TILE, HEAD_PAD), lambda i, t: (t, 0)),
        pl.BlockSpec((TILE, HEAD_PAD), lambda i, t: (t, 0)),
        pl.BlockSpec((MLA_ROPE // 2, TILE), lambda i, t: (0, t)),
        pl.BlockSpec((MLA_ROPE // 2, TILE), lambda i, t: (0, t)),
    ]
    out_specs = [
        pl.BlockSpec((1, TILE, NA_WIDTH), lambda i, t: (i, t, 0)),
        tile4(NA_WIDTH),
        tile4(NA_WIDTH),
        tile4(NA_WIDTH + MLA_WIDTH),
        tile4(HEADS * HEAD_PAD),
        pl.BlockSpec((1, TILE, HEADS * HEAD_PAD), lambda i, t: (i, t, 0)),
        pl.BlockSpec((1, MLA_WIDTH, TILE), lambda i, t: (i, 0, t)),
    ]
    out_shape = [
        jax.ShapeDtypeStruct((b, TOK, NA_WIDTH), BF16),
        jax.ShapeDtypeStruct((b, N_TILES, NA_WIDTH, TILE), BF16),
        jax.ShapeDtypeStruct((b, N_TILES, NA_WIDTH, TILE), BF16),
        jax.ShapeDtypeStruct((b, N_TILES, NA_WIDTH + MLA_WIDTH, TILE), BF16),
        jax.ShapeDtypeStruct((b, N_TILES, HEADS * HEAD_PAD, TILE), BF16),
        jax.ShapeDtypeStruct((b, TOK, HEADS * HEAD_PAD), BF16),
        jax.ShapeDtypeStruct((b, MLA_WIDTH, TOK), BF16),
    ]
    return in_specs, out_specs, out_shape


def _proj_call(lat, con, proj_args):
    b = lat.shape[0]
    in_specs, out_specs, out_shape = _proj_specs(b)
    return pl.pallas_call(
        _proj_kernel,
        grid=(b, N_TILES),
        in_specs=_stream_specs(con) + in_specs,
        out_specs=out_specs,
        out_shape=out_shape,
        compiler_params=_params("arbitrary", "arbitrary"),
        name="norm_projections",
    )(lat, con, *proj_args)


def _with_ones(vt):
    return jnp.concatenate([vt, jnp.ones((ONES_ROWS, vt.shape[1]), vt.dtype)], axis=0)


def _online_pv(chunks):
    m = o = None
    for s, v in chunks:
        s = s.astype(BF16)
        cm = jnp.max(s, axis=0, keepdims=True)
        if m is None:
            m = cm
            o = jnp.dot(v, jnp.exp2(s - m), preferred_element_type=F32)
        else:
            m_new = jnp.maximum(m, cm)
            alpha = jnp.exp2(m.astype(F32) - m_new.astype(F32))
            o = o * alpha + jnp.dot(v, jnp.exp2(s - m_new), preferred_element_type=F32)
            m = m_new
    return o


def _gated(o, gate):
    dv = gate.shape[0]
    return (o[:dv] * (1.0 / o[dv:dv + 1]) * gate.astype(F32)).astype(BF16)


def _mla_kernel(q_ref, k_ref, v_ref, g_ref, o_ref, *, out_tiles):
    chunks = [(a, min(a + KEY_CHUNK, TOK)) for a in range(0, TOK, KEY_CHUNK)]
    for hh in range(MLA_HEADS_PER_STEP):
        qr = slice(hh * HEAD_PAD, (hh + 1) * HEAD_PAD)
        vr = slice(hh * MLA_V, (hh + 1) * MLA_V)
        v_aug = _with_ones(v_ref[0, vr, :])
        for j in range(N_LAT_TILES // 2):
            q2 = jnp.concatenate([q_ref[0, 2 * j, qr, :], q_ref[0, 2 * j + 1, qr, :]], axis=1)
            s_all = jnp.dot(k_ref[0, :, qr], q2, preferred_element_type=F32).astype(BF16)
            o = _online_pv([(s_all[a:b], v_aug[:, a:b]) for a, b in chunks])
            for i in range(2):
                o_ref[0, 2 * j + i, vr, :] = _gated(o[:, i * TILE:(i + 1) * TILE], g_ref[0, 2 * j + i, vr, :])
        if out_tiles == N_TILES:
            qi = N_LAT_TILES
            s = jnp.dot(k_ref[0, SEQ:, qr], q_ref[0, qi, qr, :], preferred_element_type=F32)
            o_ref[0, qi, vr, :] = _gated(_online_pv([(s, v_aug[:, SEQ:])]), g_ref[0, qi, vr, :])


def _mla_call(qml, kml, vml, gt, out_tiles):
    b = qml.shape[0]
    n = MLA_HEADS_PER_STEP
    return pl.pallas_call(
        functools.partial(_mla_kernel, out_tiles=out_tiles),
        grid=(b, HEADS // n),
        in_specs=[
            pl.BlockSpec((1, N_TILES, n * HEAD_PAD, TILE), lambda i, h: (i, 0, h, 0)),
            pl.BlockSpec((1, TOK, n * HEAD_PAD), lambda i, h: (i, 0, h)),
            pl.BlockSpec((1, n * MLA_V, TOK), lambda i, h: (i, h, 0)),
            pl.BlockSpec((1, N_TILES, n * MLA_V, TILE), lambda i, h: (i, 0, HEADS // n + h, 0)),
        ],
        out_specs=pl.BlockSpec((1, out_tiles, n * MLA_V, TILE), lambda i, h: (i, 0, h, 0)),
        out_shape=jax.ShapeDtypeStruct((b, out_tiles, MLA_WIDTH, TILE), BF16),
        compiler_params=_params("arbitrary", "arbitrary"),
        name="mla_attention",
    )(qml, kml, vml, gt)


def _na_window(qi):
    return min(max(qi - 1, 0), N_LAT_TILES - NA_KEY_TILES)


def _na_key_tiles(qi):
    rows = SEQ // GRID_W
    tile_rows = TILE // GRID_W
    lo = min(max(qi * tile_rows - NA_WIN_H // 2, 0), rows - NA_WIN_H)
    hi = min(max((qi + 1) * tile_rows - 1 - NA_WIN_H // 2, 0), rows - NA_WIN_H) + NA_WIN_H - 1
    return list(range(lo // tile_rows, hi // tile_rows + 1))


def _na_kernel(q_ref, k_ref, v_ref, g_ref, b_ref, o_ref, *, out_tiles):
    pair = 2 * NA_DIM
    row = lax.broadcasted_iota(jnp.int32, (pair, 1), 0)
    first = row < NA_DIM

    def attend(pp, qi, key_tiles, var):
        pr = slice(pp * pair, (pp + 1) * pair)
        qp = q_ref[0, qi, pr, :]
        zero = jnp.zeros_like(qp)
        q2 = jnp.concatenate([jnp.where(first, qp, zero), jnp.where(first, zero, qp)], axis=1)
        scores = []
        for t in key_tiles:
            s = jnp.dot(k_ref[0, t * TILE:(t + 1) * TILE, pr], q2, preferred_element_type=F32)
            if t < N_LAT_TILES:
                r0 = (t - _na_window(qi)) * TILE
                s = s + jnp.concatenate([b_ref[0, var, 2 * pp, r0:r0 + TILE, :],
                                         b_ref[0, var, 2 * pp + 1, r0:r0 + TILE, :]], axis=1)
            scores.append(s)
        for i in range(2):
            hr = slice(pp * pair + i * NA_DIM, pp * pair + (i + 1) * NA_DIM)
            o = _online_pv([(s[:, i * TILE:(i + 1) * TILE], _with_ones(v_ref[0, t, hr, :]))
                            for s, t in zip(scores, key_tiles)])
            o_ref[0, qi, hr, :] = _gated(o, g_ref[0, qi, hr, :])

    for pp in range(NA_PAIRS_PER_STEP):
        for qi in range(N_LAT_TILES):
            var = 0 if qi == 0 else (2 if qi == N_LAT_TILES - 1 else 1)
            attend(pp, qi, [N_LAT_TILES] + _na_key_tiles(qi), var)
        if out_tiles == N_TILES:
            attend(pp, N_LAT_TILES, [N_LAT_TILES], 0)


def _na_call(qna, kna, vna, gt, bias_t, layer, out_tiles):
    b = qna.shape[0]
    rows = NA_PAIRS_PER_STEP * 2 * NA_DIM
    return pl.pallas_call(
        functools.partial(_na_kernel, out_tiles=out_tiles),
        grid=(NA_WIDTH // rows, b),
        in_specs=[
            pl.BlockSpec((1, N_TILES, rows, TILE), lambda h, i: (i, 0, h, 0)),
            pl.BlockSpec((1, TOK, rows), lambda h, i: (i, 0, h)),
            pl.BlockSpec((1, N_TILES, rows, TILE), lambda h, i: (i, 0, h, 0)),
            pl.BlockSpec((1, N_TILES, rows, TILE), lambda h, i: (i, 0, h, 0)),
            pl.BlockSpec((1, 3, 2 * NA_PAIRS_PER_STEP, NA_LOCAL_KEYS, TILE), lambda h, i: (layer, 0, h, 0, 0)),
        ],
        out_specs=pl.BlockSpec((1, out_tiles, rows, TILE), lambda h, i: (i, 0, h, 0)),
        out_shape=jax.ShapeDtypeStruct((b, out_tiles, NA_WIDTH, TILE), BF16),
        compiler_params=_params("arbitrary", "arbitrary"),
        name="na_attention",
    )(qna, kna, vna, gt, bias_t)


def _residual(ona, oml, wa_ref, wb_ref, x, gate):
    y = (lax.dot_general(ona, wa_ref[...], _TN, preferred_element_type=F32)
         + lax.dot_general(oml, wb_ref[...], _TN, preferred_element_type=F32))
    return x + gate * y


def _out_proj_kernel(ona_ref, oml_ref, wa_ref, wb_ref, lat_ref, con_ref, gmod_ref, *refs):
    x_ref = refs[-8]
    xn = _residual(ona_ref[0, 0], oml_ref[0, 0], wa_ref, wb_ref, _stream_tile(lat_ref, con_ref),
                   gmod_ref[0, 0, 2:3, :])
    x_ref[0] = xn
    _project(xn, *refs[:-8], *refs[-7:])


def _out_proj_call(ona, oml, wa, wb, lat, con, mod, proj_args):
    b = lat.shape[0]
    in_specs, out_specs, out_shape = _proj_specs(b)
    att = lambda width: pl.BlockSpec((1, 1, width, TILE), lambda i, t: (i, t, 0, 0))
    return pl.pallas_call(
        _out_proj_kernel,
        grid=(b, N_TILES),
        in_specs=[att(NA_WIDTH), att(MLA_WIDTH), _const_spec((NA_WIDTH, D_MODEL)), _const_spec((MLA_WIDTH, D_MODEL))]
        + _stream_specs(con) + [_mod_spec()] + in_specs,
        out_specs=[pl.BlockSpec((1, TILE, D_MODEL), lambda i, t: (i, t, 0))] + out_specs,
        out_shape=[jax.ShapeDtypeStruct((b, TOK, D_MODEL), F32)] + out_shape,
        compiler_params=_params("arbitrary", "arbitrary"),
        name="out_projection_next_projections",
    )(ona, oml, wa, wb, lat, con, mod, *proj_args)


def _final_kernel(ona_ref, oml_ref, wa_ref, wb_ref, x_ref, mod_ref, fg_ref, o_ref):
    gate = mod_ref[0, 0, 2:3, :]
    for u in range(FINAL_TILES):
        rows = slice(u * TILE, (u + 1) * TILE)
        xn = _residual(ona_ref[0, u], oml_ref[0, u], wa_ref, wb_ref, x_ref[0, rows, :], gate)
        o_ref[0, rows, :] = _rms(xn, fg_ref[...])


def _final_call(ona, oml, wa, wb, stream, mod, fg):
    b = stream.shape[0]
    att = lambda width: pl.BlockSpec((1, FINAL_TILES, width, TILE), lambda i, t: (i, t, 0, 0))
    rows = FINAL_TILES * TILE
    return pl.pallas_call(
        _final_kernel,
        grid=(b, N_LAT_TILES // FINAL_TILES),
        in_specs=[att(NA_WIDTH), att(MLA_WIDTH), _const_spec((NA_WIDTH, D_MODEL)), _const_spec((MLA_WIDTH, D_MODEL)),
                  pl.BlockSpec((1, rows, D_MODEL), lambda i, t: (i, t, 0)),
                  pl.BlockSpec((1, 1, 3, D_MODEL), lambda i, t: (i, 0, 0, 0)),
                  _const_spec((1, D_MODEL))],
        out_specs=pl.BlockSpec((1, rows, D_MODEL), lambda i, t: (i, t, 0)),
        out_shape=jax.ShapeDtypeStruct((b, SEQ, D_MODEL), F32),
        compiler_params=_params("arbitrary", "arbitrary"),
        name="out_projection_final_norm",
    )(ona, oml, wa, wb, stream, mod, fg)


def _rope_tables():
    t = np.arange(SEQ)
    row = (t // GRID_W).astype(np.float32)
    col = (t % GRID_W).astype(np.float32)
    per_axis = MLA_ROPE // 2
    inv = (1.0 / (ROPE_THETA ** (jnp.arange(0, per_axis, 2, dtype=F32) / per_axis)))
    ang = jnp.concatenate([jnp.asarray(row)[:, None] * inv[None], jnp.asarray(col)[:, None] * inv[None]], axis=-1)
    cos = jnp.concatenate([jnp.cos(ang), jnp.ones((CTX, per_axis), F32)], axis=0)
    sin = jnp.concatenate([jnp.sin(ang), jnp.zeros((CTX, per_axis), F32)], axis=0)
    zeros = jnp.zeros((TOK, HEAD_PAD - 2 * per_axis), F32)
    zh = jnp.zeros((TOK, per_axis), F32)
    ck = jnp.concatenate([cos, cos, zeros], axis=1)
    sa = jnp.concatenate([-sin, zh, zeros], axis=1)
    sb = jnp.concatenate([zh, sin, zeros], axis=1)
    return ck, sa, sb, cos.T, sin.T


def _col_tables():
    k_c = np.arange(GRID_W)[:, None]
    q_c = np.arange(GRID_W)[None, :]
    cs = np.clip(q_c - NA_WIN_W // 2, 0, GRID_W - NA_WIN_W)
    dcol = np.clip(k_c - q_c + NA_WIN_W - 1, 0, 2 * NA_WIN_W - 2).reshape(-1)
    sel = (np.arange(RPB_COLS_PAD)[:, None] == dcol[None, :]).astype(np.float32)
    ok = ((k_c >= cs) & (k_c < cs + NA_WIN_W)).astype(np.float32).reshape(1, -1)
    return jnp.asarray(sel, BF16), jnp.asarray(ok)


def _bias_kernel(rpb_ref, sel_ref, ok_ref, o_ref):
    v = rpb_ref[...] * LOG2E
    hi = v.astype(BF16)
    r1 = v - hi.astype(F32)
    mid = r1.astype(BF16)
    lo = (r1 - mid.astype(F32)).astype(BF16)
    sel = sel_ref[...]
    t = (jnp.dot(hi, sel, preferred_element_type=F32) + jnp.dot(mid, sel, preferred_element_type=F32)
         + jnp.dot(lo, sel, preferred_element_type=F32))
    o_ref[...] = jnp.where(ok_ref[...] > 0.5, t, MASK_VALUE)


def _bias_rows_kernel(toe_ref, o_ref):
    rows = SEQ // GRID_W
    tile_rows = TILE // GRID_W
    masked = jnp.full((GRID_W, GRID_W), MASK_VALUE, F32)
    for v, g in enumerate((0, 1, N_LAT_TILES - 1)):
        for kr in range(NA_KEY_TILES * tile_rows):
            k_r = _na_window(g) * tile_rows + kr
            blocks = []
            for qi in range(tile_rows):
                q_r = g * tile_rows + qi
                rs = min(max(q_r - NA_WIN_H // 2, 0), rows - NA_WIN_H)
                inside = rs <= k_r < rs + NA_WIN_H
                blocks.append(toe_ref[0, k_r - q_r + NA_WIN_H - 1] if inside else masked)
            o_ref[0, v, 0, kr * GRID_W:(kr + 1) * GRID_W, :] = jnp.concatenate(blocks, axis=1)


def _na_bias(rpb):
    sel, ok = _col_tables()
    lh = rpb.shape[0] * HEADS
    pad = jnp.zeros((lh, RPB_ROWS_PAD, RPB_COLS_PAD), F32)
    pad = pad.at[:, :2 * NA_WIN_H - 1, :2 * NA_WIN_W - 1].set(rpb.reshape(lh, 2 * NA_WIN_H - 1, 2 * NA_WIN_W - 1))
    toe = pl.pallas_call(
        _bias_kernel,
        out_shape=jax.ShapeDtypeStruct((lh * RPB_ROWS_PAD, GRID_W * GRID_W), F32),
        compiler_params=pltpu.CompilerParams(vmem_limit_bytes=VMEM_LIMIT_BYTES),
        name="na_bias_columns",
    )(pad.reshape(lh * RPB_ROWS_PAD, RPB_COLS_PAD), sel, ok)
    toe = toe.reshape(lh, RPB_ROWS_PAD, GRID_W, GRID_W)
    return pl.pallas_call(
        _bias_rows_kernel,
        grid=(lh,),
        in_specs=[pl.BlockSpec((1, RPB_ROWS_PAD, GRID_W, GRID_W), lambda n: (n, 0, 0, 0))],
        out_specs=pl.BlockSpec((1, 3, 1, NA_LOCAL_KEYS, TILE), lambda n: (n // HEADS, 0, n % HEADS, 0, 0)),
        out_shape=jax.ShapeDtypeStruct((rpb.shape[0], 3, HEADS, NA_LOCAL_KEYS, TILE), F32),
        compiler_params=_params("arbitrary"),
        name="na_bias_rows",
    )(toe)


def _layer_weights(w_in, w_uq, w_ukv, w_out):
    cuts = np.cumsum([NA_WIDTH] * 4 + [Q_RANK, KV_RANK, MLA_ROPE, MLA_WIDTH])
    na_q, na_k, na_v, na_g, c_q, c_kv, k_r, ml_g = jnp.split(w_in, [int(c) for c in cuts[:-1]], axis=1)
    kr_blk = jnp.concatenate([k_r[:, 0::2], k_r[:, 1::2], jnp.zeros((D_MODEL, HEAD_PAD - MLA_ROPE), F32)], axis=1)
    wtok = jnp.concatenate([na_k, c_q, c_kv, kr_blk], axis=1).astype(BF16)
    wt = jnp.concatenate([na_q, na_v, na_g, ml_g], axis=1).T.astype(BF16)

    uq = w_uq.reshape(Q_RANK, HEADS, MLA_NOPE + MLA_ROPE)
    uq = jnp.concatenate([uq[..., :MLA_NOPE], uq[..., MLA_NOPE::2], uq[..., MLA_NOPE + 1::2],
                          jnp.zeros((Q_RANK, HEADS, HEAD_PAD - MLA_NOPE - MLA_ROPE), F32)], axis=-1)
    wq = uq.reshape(Q_RANK, HEADS * HEAD_PAD).T.astype(BF16)

    ukv = w_ukv.reshape(KV_RANK, HEADS, MLA_NOPE + MLA_V)
    k_top = jnp.concatenate([ukv[..., :MLA_NOPE], jnp.zeros((KV_RANK, HEADS, HEAD_PAD - MLA_NOPE), F32)], axis=-1)
    eye = jnp.concatenate([jnp.zeros((MLA_ROPE, MLA_NOPE), F32), jnp.eye(MLA_ROPE, dtype=F32),
                           jnp.zeros((MLA_ROPE, HEAD_PAD - MLA_NOPE - MLA_ROPE), F32)], axis=1)
    k_bot = jnp.concatenate([jnp.broadcast_to(eye[:, None], (MLA_ROPE, HEADS, HEAD_PAD)),
                             jnp.zeros((KV_RANK - MLA_ROPE, HEADS, HEAD_PAD), F32)], axis=0)
    wk = jnp.concatenate([k_top, k_bot], axis=0).reshape(2 * KV_RANK, HEADS * HEAD_PAD).astype(BF16)
    wv = ukv[..., MLA_NOPE:].reshape(KV_RANK, MLA_WIDTH).T.astype(BF16)

    wa = w_out[:NA_WIDTH].astype(BF16)
    wb = w_out[NA_WIDTH:].astype(BF16)
    return wtok, wt, wq, wk, wv, wa, wb


def kernel(x, c, ctx, c_ctx, norm_g, w_ada, b_ada, w_in, na_rpb, q_norm_g, w_uq, kv_norm_g, w_ukv, w_out,
           final_norm_g):
    b = x.shape[0]
    tabs = _rope_tables()

    rows = -(-(b + 1) // 8) * 8
    cc = jnp.concatenate([c, c_ctx[None], jnp.zeros((rows - b - 1, D_MODEL), F32)], axis=0)
    ada = _ada_call(cc, w_ada, b_ada)
    bias_t = _na_bias(na_rpb)

    def layer_inputs(l):
        mod_lat = ada[l, :b].reshape(b, 1, 3, D_MODEL)
        mod_con = jnp.broadcast_to(ada[l, b].reshape(1, 1, 3, D_MODEL), (b, 1, 3, D_MODEL))
        mod = jnp.concatenate([mod_lat, mod_con], axis=1)
        wtok, wt, wq, wk, wv, wa, wb = _layer_weights(w_in[l], w_uq[l], w_ukv[l], w_out[l])
        proj_args = (mod, norm_g[l].reshape(1, D_MODEL), wtok, wt, q_norm_g[l].reshape(1, Q_RANK), wq,
                     kv_norm_g[l].reshape(1, KV_RANK), wk, wv, *tabs)
        return mod, proj_args, wa, wb

    lat, con = x, ctx
    mod, proj_args, wa, wb = layer_inputs(0)
    proj = _proj_call(lat, con, proj_args)
    for l in range(DEPTH):
        last = l == DEPTH - 1
        kna, qna, vna, gt, qml, kml, vml = proj
        out_tiles = N_LAT_TILES if last else N_TILES
        ona = _na_call(qna, kna, vna, gt, bias_t, l, out_tiles)
        oml = _mla_call(qml, kml, vml, gt, out_tiles)
        if last:
            return _final_call(ona, oml, wa, wb, lat, mod, final_norm_g.reshape(1, D_MODEL))
        nxt_mod, nxt_args, nxt_wa, nxt_wb = layer_inputs(l + 1)
        stream, *proj = _out_proj_call(ona, oml, wa, wb, lat, con, mod, nxt_args)
        lat = con = stream
        mod, wa, wb = nxt_mod, nxt_wa, nxt_wb
```

```python
import functools
import math

import numpy as np
import jax
import jax.numpy as jnp
from jax import lax
from jax.experimental import pallas as pl
from jax.experimental.pallas import tpu as pltpu

D_MODEL = 1024
SEQ = 2048
CTX = 256
TOK = SEQ + CTX
GRID_W = 64
EPS = 1e-6
DEPTH = 2

HEADS = 8
NA_DIM = 64
NA_WIDTH = HEADS * NA_DIM
NA_WIN_H = 8
NA_WIN_W = 16
MLA_NOPE = 64
MLA_ROPE = 32
MLA_V = 64
MLA_WIDTH = HEADS * MLA_V
Q_RANK = 256
KV_RANK = 128
ROPE_THETA = 10000.0
LOG2E = math.log2(math.e)
NA_QSCALE = NA_DIM ** -0.5 * LOG2E
MLA_QSCALE = (MLA_NOPE + MLA_ROPE) ** -0.5 * LOG2E
MASK_VALUE = -1e30

TILE = 256
N_LAT_TILES = SEQ // TILE
N_TILES = TOK // TILE
HEAD_PAD = 128
ONES_ROWS = 16
NA_KEY_TILES = 3
NA_LOCAL_KEYS = NA_KEY_TILES * TILE
KEY_CHUNK = 256
FINAL_TILES = 4
MLA_HEADS_PER_STEP = 4
NA_PAIRS_PER_STEP = 2
RPB_ROWS_PAD = 16
RPB_COLS_PAD = 32

VMEM_LIMIT_BYTES = 48 * 1024 * 1024

BF16 = jnp.bfloat16
F32 = jnp.float32

_NT = (((1,), (1,)), ((), ()))
_TN = (((0,), (0,)), ((), ()))


def _params(*semantics):
    return pltpu.CompilerParams(dimension_semantics=semantics, vmem_limit_bytes=VMEM_LIMIT_BYTES)


def _rms(v, g):
    return v * lax.rsqrt(jnp.mean(v * v, axis=-1, keepdims=True) + EPS) * g


def _ada_kernel(c_ref, w_ref, b_ref, o_ref):
    cv = c_ref[...]
    a = (cv * jax.nn.sigmoid(cv)).astype(BF16)
    o_ref[0] = jnp.dot(a, w_ref[0].astype(BF16), preferred_element_type=F32) + b_ref[0]


def _ada_call(cc, w_ada, b_ada):
    rows = cc.shape[0]
    return pl.pallas_call(
        _ada_kernel,
        grid=(DEPTH, 3),
        in_specs=[
            pl.BlockSpec((rows, D_MODEL), lambda l, j: (0, 0)),
            pl.BlockSpec((1, D_MODEL, D_MODEL), lambda l, j: (l, 0, j)),
            pl.BlockSpec((1, 1, D_MODEL), lambda l, j: (l, 0, j)),
        ],
        out_specs=pl.BlockSpec((1, rows, D_MODEL), lambda l, j: (l, 0, j)),
        out_shape=jax.ShapeDtypeStruct((DEPTH, rows, 3 * D_MODEL), F32),
        compiler_params=_params("arbitrary", "arbitrary"),
        name="ada_modulation",
    )(cc, w_ada, b_ada.reshape(DEPTH, 1, 3 * D_MODEL))


def _stream_specs(con):
    con_block = con.shape[1] // CTX - 1
    return [pl.BlockSpec((1, TILE, D_MODEL), lambda i, t: (i, jnp.minimum(t, N_LAT_TILES - 1), 0)),
            pl.BlockSpec((1, CTX, D_MODEL), lambda i, t: (i, con_block, 0))]


def _stream_tile(lat_ref, con_ref):
    return jnp.where(pl.program_id(1) == N_LAT_TILES, con_ref[0], lat_ref[0])


def _project(x, mod_ref, ng_ref, wtok_ref, wt_ref, qg_ref, wq_ref, kvg_ref, wk_ref, wv_ref,
             ck_ref, sa_ref, sb_ref, cos_ref, sin_ref,
             kna_ref, qna_ref, vna_ref, gt_ref, qml_ref, kml_ref, vml_ref):
    shift = mod_ref[0, 0, 0:1, :]
    scale = mod_ref[0, 0, 1:2, :]
    h = _rms(x, ng_ref[...]) * (1.0 + scale) + shift
    hb = h.astype(BF16)
    r_tok = jnp.dot(hb, wtok_ref[...], preferred_element_type=F32)
    r_t = lax.dot_general(wt_ref[...], hb, _NT, preferred_element_type=F32)

    kna_ref[0] = r_tok[:, 0:NA_WIDTH].astype(BF16)
    qna_ref[0, 0] = (r_t[0:NA_WIDTH] * NA_QSCALE).astype(BF16)
    vna_ref[0, 0] = r_t[NA_WIDTH:2 * NA_WIDTH].astype(BF16)
    gates = r_t[2 * NA_WIDTH:]
    gt_ref[0, 0] = (gates * jax.nn.sigmoid(gates)).astype(BF16)

    c0 = NA_WIDTH
    cq = _rms(r_tok[:, c0:c0 + Q_RANK], qg_ref[...]).astype(BF16)
    ckv = _rms(r_tok[:, c0 + Q_RANK:c0 + Q_RANK + KV_RANK], kvg_ref[...]).astype(BF16)
    kr = r_tok[:, c0 + Q_RANK + KV_RANK:]

    q_t = lax.dot_general(wq_ref[...], cq, _NT, preferred_element_type=F32) * MLA_QSCALE
    cos_t = cos_ref[...]
    sin_t = sin_ref[...]
    half = MLA_ROPE // 2
    for hd in range(HEADS):
        r0 = hd * HEAD_PAD
        x1 = q_t[r0 + MLA_NOPE:r0 + MLA_NOPE + half]
        x2 = q_t[r0 + MLA_NOPE + half:r0 + MLA_NOPE + MLA_ROPE]
        qml_ref[0, 0, r0:r0 + MLA_NOPE, :] = q_t[r0:r0 + MLA_NOPE].astype(BF16)
        qml_ref[0, 0, r0 + MLA_NOPE:r0 + MLA_NOPE + half, :] = (x1 * cos_t - x2 * sin_t).astype(BF16)
        qml_ref[0, 0, r0 + MLA_NOPE + half:r0 + MLA_NOPE + MLA_ROPE, :] = (x1 * sin_t + x2 * cos_t).astype(BF16)
        qml_ref[0, 0, r0 + MLA_NOPE + MLA_ROPE:r0 + HEAD_PAD, :] = q_t[r0 + MLA_NOPE + MLA_ROPE:r0 + HEAD_PAD].astype(BF16)

    krr = (kr * ck_ref[...] + pltpu.roll(kr, HEAD_PAD - half, 1) * sa_ref[...]
           + pltpu.roll(kr, half, 1) * sb_ref[...]).astype(BF16)
    lhs = jnp.concatenate([ckv, krr], axis=1)
    kml_ref[0] = jnp.dot(lhs, wk_ref[...], preferred_element_type=F32).astype(BF16)
    vml_ref[0] = lax.dot_general(wv_ref[...], ckv, _NT, preferred_element_type=F32).astype(BF16)


def _proj_kernel(lat_ref, con_ref, *refs):
    _project(_stream_tile(lat_ref, con_ref), *refs)


def _const_spec(shape):
    return pl.BlockSpec(shape, lambda i, t: (0,) * len(shape))


def _mod_spec():
    return pl.BlockSpec((1, 1, 3, D_MODEL), lambda i, t: (i, t // N_LAT_TILES, 0, 0))


def _proj_specs(b):
    const = _const_spec
    tile4 = lambda rows: pl.BlockSpec((1, 1, rows, TILE), lambda i, t: (i, t, 0, 0))
    in_specs = [
        _mod_spec(),
        const((1, D_MODEL)),
        const((D_MODEL, 1024)),
        const((2048, D_MODEL)),
        const((1, Q_RANK)),
        const((HEADS * HEAD_PAD, Q_RANK)),
        const((1, KV_RANK)),
        const((2 * KV_RANK, HEADS * HEAD_PAD)),
        const((MLA_WIDTH, KV_RANK)),
        pl.BlockSpec((TILE, HEAD_PAD), lambda i, t: (t, 0)),
        pl.BlockSpec((TILE, HEAD_PAD), lambda i, t: (t, 0)),
        pl.BlockSpec((TILE, HEAD_PAD), lambda i, t: (t, 0)),
        pl.BlockSpec((MLA_ROPE // 2, TILE), lambda i, t: (0, t)),
        pl.BlockSpec((MLA_ROPE // 2, TILE), lambda i, t: (0, t)),
    ]
    out_specs = [
        pl.BlockSpec((1, TILE, NA_WIDTH), lambda i, t: (i, t, 0)),
        tile4(NA_WIDTH),
        tile4(NA_WIDTH),
        tile4(NA_WIDTH + MLA_WIDTH),
        tile4(HEADS * HEAD_PAD),
        pl.BlockSpec((1, TILE, HEADS * HEAD_PAD), lambda i, t: (i, t, 0)),
        pl.BlockSpec((1, MLA_WIDTH, TILE), lambda i, t: (i, 0, t)),
    ]
    out_shape = [
        jax.ShapeDtypeStruct((b, TOK, NA_WIDTH), BF16),
        jax.ShapeDtypeStruct((b, N_TILES, NA_WIDTH, TILE), BF16),
        jax.ShapeDtypeStruct((b, N_TILES, NA_WIDTH, TILE), BF16),
        jax.ShapeDtypeStruct((b, N_TILES, NA_WIDTH + MLA_WIDTH, TILE), BF16),
        jax.ShapeDtypeStruct((b, N_TILES, HEADS * HEAD_PAD, TILE), BF16),
        jax.ShapeDtypeStruct((b, TOK, HEADS * HEAD_PAD), BF16),
        jax.ShapeDtypeStruct((b, MLA_WIDTH, TOK), BF16),
    ]
    return in_specs, out_specs, out_shape


def _proj_call(lat, con, proj_args):
    b = lat.shape[0]
    in_specs, out_specs, out_shape = _proj_specs(b)
    return pl.pallas_call(
        _proj_kernel,
        grid=(b, N_TILES),
        in_specs=_stream_specs(con) + in_specs,
        out_specs=out_specs,
        out_shape=out_shape,
        compiler_params=_params("arbitrary", "arbitrary"),
        name="norm_projections",
    )(lat, con, *proj_args)


def _with_ones(vt):
    return jnp.concatenate([vt, jnp.ones((ONES_ROWS, vt.shape[1]), vt.dtype)], axis=0)


def _online_pv(chunks):
    m = o = None
    for s, v in chunks:
        s = s.astype(BF16)
        cm = jnp.max(s, axis=0, keepdims=True)
        if m is None:
            m = cm
            o = jnp.dot(v, jnp.exp2(s - m), preferred_element_type=F32)
        else:
            m_new = jnp.maximum(m, cm)
            alpha = jnp.exp2(m.astype(F32) - m_new.astype(F32))
            o = o * alpha + jnp.dot(v, jnp.exp2(s - m_new), preferred_element_type=F32)
            m = m_new
    return o


def _gated(o, gate):
    dv = gate.shape[0]
    return (o[:dv] * (1.0 / o[dv:dv + 1]) * gate.astype(F32)).astype(BF16)


def _mla_kernel(q_ref, k_ref, v_ref, g_ref, o_ref):
    chunks = [(a, min(a + KEY_CHUNK, TOK)) for a in range(0, TOK, KEY_CHUNK)]
    heads = [(slice(hh * HEAD_PAD, (hh + 1) * HEAD_PAD), slice(hh * MLA_V, (hh + 1) * MLA_V))
             for hh in range(MLA_HEADS_PER_STEP)]
    for qr, vr in heads:
        s = jnp.dot(k_ref[0, SEQ:, qr], q_ref[0, N_LAT_TILES, qr, :], preferred_element_type=F32)
        o = _online_pv([(s, _with_ones(v_ref[0, vr, SEQ:]))])
        o_ref[0, N_LAT_TILES, vr, :] = _gated(o, g_ref[0, N_LAT_TILES, vr, :])
    for qr, vr in heads:
        v_aug = _with_ones(v_ref[0, vr, :])
        for j in range(N_LAT_TILES // 2):
            q2 = jnp.concatenate([q_ref[0, 2 * j, qr, :], q_ref[0, 2 * j + 1, qr, :]], axis=1)
            s_all = jnp.dot(k_ref[0, :, qr], q2, preferred_element_type=F32).astype(BF16)
            o = _online_pv([(s_all[a:b], v_aug[:, a:b]) for a, b in chunks])
            for i in range(2):
                o_ref[0, 2 * j + i, vr, :] = _gated(o[:, i * TILE:(i + 1) * TILE], g_ref[0, 2 * j + i, vr, :])


def _mla_call(qml, kml, vml, gt):
    b = qml.shape[0]
    n = MLA_HEADS_PER_STEP
    out_tiles = N_TILES
    return pl.pallas_call(
        _mla_kernel,
        grid=(b, HEADS // n),
        in_specs=[
            pl.BlockSpec((1, N_TILES, n * HEAD_PAD, TILE), lambda i, h: (i, 0, h, 0)),
            pl.BlockSpec((1, TOK, n * HEAD_PAD), lambda i, h: (i, 0, h)),
            pl.BlockSpec((1, n * MLA_V, TOK), lambda i, h: (i, h, 0)),
            pl.BlockSpec((1, N_TILES, n * MLA_V, TILE), lambda i, h: (i, 0, HEADS // n + h, 0)),
        ],
        out_specs=pl.BlockSpec((1, out_tiles, n * MLA_V, TILE), lambda i, h: (i, 0, h, 0)),
        out_shape=jax.ShapeDtypeStruct((b, out_tiles, MLA_WIDTH, TILE), BF16),
        compiler_params=_params("arbitrary", "arbitrary"),
        name="mla_attention",
    )(qml, kml, vml, gt)


def _na_window(qi):
    return min(max(qi - 1, 0), N_LAT_TILES - NA_KEY_TILES)


def _na_key_tiles(qi):
    rows = SEQ // GRID_W
    tile_rows = TILE // GRID_W
    lo = min(max(qi * tile_rows - NA_WIN_H // 2, 0), rows - NA_WIN_H)
    hi = min(max((qi + 1) * tile_rows - 1 - NA_WIN_H // 2, 0), rows - NA_WIN_H) + NA_WIN_H - 1
    return list(range(lo // tile_rows, hi // tile_rows + 1))


def _na_kernel(q_ref, k_ref, v_ref, g_ref, b_ref, o_ref, *, out_tiles):
    pair = 2 * NA_DIM
    row = lax.broadcasted_iota(jnp.int32, (pair, 1), 0)
    first = row < NA_DIM

    def attend(pp, qi, key_tiles, var):
        pr = slice(pp * pair, (pp + 1) * pair)
        qp = q_ref[0, qi, pr, :]
        zero = jnp.zeros_like(qp)
        q2 = jnp.concatenate([jnp.where(first, qp, zero), jnp.where(first, zero, qp)], axis=1)
        scores = []
        for t in key_tiles:
            s = jnp.dot(k_ref[0, t * TILE:(t + 1) * TILE, pr], q2, preferred_element_type=F32)
            if t < N_LAT_TILES:
                r0 = (t - _na_window(qi)) * TILE
                s = s + jnp.concatenate([b_ref[0, var, 2 * pp, r0:r0 + TILE, :],
                                         b_ref[0, var, 2 * pp + 1, r0:r0 + TILE, :]], axis=1)
            scores.append(s)
        for i in range(2):
            hr = slice(pp * pair + i * NA_DIM, pp * pair + (i + 1) * NA_DIM)
            o = _online_pv([(s[:, i * TILE:(i + 1) * TILE], _with_ones(v_ref[0, t, hr, :]))
                            for s, t in zip(scores, key_tiles)])
            o_ref[0, qi, hr, :] = _gated(o, g_ref[0, qi, hr, :])

    for pp in range(NA_PAIRS_PER_STEP):
        for qi in range(N_LAT_TILES):
            var = 0 if qi == 0 else (2 if qi == N_LAT_TILES - 1 else 1)
            attend(pp, qi, [N_LAT_TILES] + _na_key_tiles(qi), var)
        if out_tiles == N_TILES:
            attend(pp, N_LAT_TILES, [N_LAT_TILES], 0)


def _na_call(qna, kna, vna, gt, bias_t, layer, out_tiles):
    b = qna.shape[0]
    rows = NA_PAIRS_PER_STEP * 2 * NA_DIM
    return pl.pallas_call(
        functools.partial(_na_kernel, out_tiles=out_tiles),
        grid=(NA_WIDTH // rows, b),
        in_specs=[
            pl.BlockSpec((1, N_TILES, rows, TILE), lambda h, i: (i, 0, h, 0)),
            pl.BlockSpec((1, TOK, rows), lambda h, i: (i, 0, h)),
            pl.BlockSpec((1, N_TILES, rows, TILE), lambda h, i: (i, 0, h, 0)),
            pl.BlockSpec((1, N_TILES, rows, TILE), lambda h, i: (i, 0, h, 0)),
            pl.BlockSpec((1, 3, 2 * NA_PAIRS_PER_STEP, NA_LOCAL_KEYS, TILE), lambda h, i: (layer, 0, h, 0, 0)),
        ],
        out_specs=pl.BlockSpec((1, out_tiles, rows, TILE), lambda h, i: (i, 0, h, 0)),
        out_shape=jax.ShapeDtypeStruct((b, out_tiles, NA_WIDTH, TILE), BF16),
        compiler_params=_params("arbitrary", "arbitrary"),
        name="na_attention",
    )(qna, kna, vna, gt, bias_t)


def _residual(ona, oml, wa_ref, wb_ref, x, gate):
    y = (lax.dot_general(ona, wa_ref[...], _TN, preferred_element_type=F32)
         + lax.dot_general(oml, wb_ref[...], _TN, preferred_element_type=F32))
    return x + gate * y


def _out_proj_kernel(ona_ref, oml_ref, wa_ref, wb_ref, lat_ref, con_ref, gmod_ref, *refs):
    x_ref = refs[-8]
    xn = _residual(ona_ref[0, 0], oml_ref[0, 0], wa_ref, wb_ref, _stream_tile(lat_ref, con_ref),
                   gmod_ref[0, 0, 2:3, :])
    x_ref[0] = xn
    _project(xn, *refs[:-8], *refs[-7:])


def _out_proj_call(ona, oml, wa, wb, lat, con, mod, proj_args):
    b = lat.shape[0]
    in_specs, out_specs, out_shape = _proj_specs(b)
    att = lambda width: pl.BlockSpec((1, 1, width, TILE), lambda i, t: (i, t, 0, 0))
    return pl.pallas_call(
        _out_proj_kernel,
        grid=(b, N_TILES),
        in_specs=[att(NA_WIDTH), att(MLA_WIDTH), _const_spec((NA_WIDTH, D_MODEL)), _const_spec((MLA_WIDTH, D_MODEL))]
        + _stream_specs(con) + [_mod_spec()] + in_specs,
        out_specs=[pl.BlockSpec((1, TILE, D_MODEL), lambda i, t: (i, t, 0))] + out_specs,
        out_shape=[jax.ShapeDtypeStruct((b, TOK, D_MODEL), F32)] + out_shape,
        compiler_params=_params("arbitrary", "arbitrary"),
        name="out_projection_next_projections",
    )(ona, oml, wa, wb, lat, con, mod, *proj_args)


def _final_kernel(ona_ref, oml_ref, wa_ref, wb_ref, x_ref, mod_ref, fg_ref, o_ref):
    gate = mod_ref[0, 0, 2:3, :]
    for u in range(FINAL_TILES):
        rows = slice(u * TILE, (u + 1) * TILE)
        xn = _residual(ona_ref[0, u], oml_ref[0, u], wa_ref, wb_ref, x_ref[0, rows, :], gate)
        o_ref[0, rows, :] = _rms(xn, fg_ref[...])


def _final_call(ona, oml, wa, wb, stream, mod, fg):
    b = stream.shape[0]
    att = lambda width: pl.BlockSpec((1, FINAL_TILES, width, TILE), lambda i, t: (i, t, 0, 0))
    rows = FINAL_TILES * TILE
    return pl.pallas_call(
        _final_kernel,
        grid=(b, N_LAT_TILES // FINAL_TILES),
        in_specs=[att(NA_WIDTH), att(MLA_WIDTH), _const_spec((NA_WIDTH, D_MODEL)), _const_spec((MLA_WIDTH, D_MODEL)),
                  pl.BlockSpec((1, rows, D_MODEL), lambda i, t: (i, t, 0)),
                  pl.BlockSpec((1, 1, 3, D_MODEL), lambda i, t: (i, 0, 0, 0)),
                  _const_spec((1, D_MODEL))],
        out_specs=pl.BlockSpec((1, rows, D_MODEL), lambda i, t: (i, t, 0)),
        out_shape=jax.ShapeDtypeStruct((b, SEQ, D_MODEL), F32),
        compiler_params=_params("arbitrary", "arbitrary"),
        name="out_projection_final_norm",
    )(ona, oml, wa, wb, stream, mod, fg)


def _rope_tables():
    t = np.arange(SEQ)
    row = (t // GRID_W).astype(np.float32)
    col = (t % GRID_W).astype(np.float32)
    per_axis = MLA_ROPE // 2
    inv = (1.0 / (ROPE_THETA ** (jnp.arange(0, per_axis, 2, dtype=F32) / per_axis)))
    ang = jnp.concatenate([jnp.asarray(row)[:, None] * inv[None], jnp.asarray(col)[:, None] * inv[None]], axis=-1)
    cos = jnp.concatenate([jnp.cos(ang), jnp.ones((CTX, per_axis), F32)], axis=0)
    sin = jnp.concatenate([jnp.sin(ang), jnp.zeros((CTX, per_axis), F32)], axis=0)
    zeros = jnp.zeros((TOK, HEAD_PAD - 2 * per_axis), F32)
    zh = jnp.zeros((TOK, per_axis), F32)
    ck = jnp.concatenate([cos, cos, zeros], axis=1)
    sa = jnp.concatenate([-sin, zh, zeros], axis=1)
    sb = jnp.concatenate([zh, sin, zeros], axis=1)
    return ck, sa, sb, cos.T, sin.T


def _col_tables():
    k_c = np.arange(GRID_W)[:, None]
    q_c = np.arange(GRID_W)[None, :]
    cs = np.clip(q_c - NA_WIN_W // 2, 0, GRID_W - NA_WIN_W)
    dcol = np.clip(k_c - q_c + NA_WIN_W - 1, 0, 2 * NA_WIN_W - 2).reshape(-1)
    sel = (np.arange(RPB_COLS_PAD)[:, None] == dcol[None, :]).astype(np.float32)
    ok = ((k_c >= cs) & (k_c < cs + NA_WIN_W)).astype(np.float32).reshape(1, -1)
    return jnp.asarray(sel, BF16), jnp.asarray(ok)


def _bias_kernel(rpb_ref, sel_ref, ok_ref, o_ref):
    v = rpb_ref[...] * LOG2E
    hi = v.astype(BF16)
    r1 = v - hi.astype(F32)
    mid = r1.astype(BF16)
    lo = (r1 - mid.astype(F32)).astype(BF16)
    sel = sel_ref[...]
    t = (jnp.dot(hi, sel, preferred_element_type=F32) + jnp.dot(mid, sel, preferred_element_type=F32)
         + jnp.dot(lo, sel, preferred_element_type=F32))
    o_ref[...] = jnp.where(ok_ref[...] > 0.5, t, MASK_VALUE)


def _bias_rows_kernel(toe_ref, o_ref):
    rows = SEQ // GRID_W
    tile_rows = TILE // GRID_W
    masked = jnp.full((GRID_W, GRID_W), MASK_VALUE, F32)
    for v, g in enumerate((0, 1, N_LAT_TILES - 1)):
        for kr in range(NA_KEY_TILES * tile_rows):
            k_r = _na_window(g) * tile_rows + kr
            blocks = []
            for qi in range(tile_rows):
                q_r = g * tile_rows + qi
                rs = min(max(q_r - NA_WIN_H // 2, 0), rows - NA_WIN_H)
                inside = rs <= k_r < rs + NA_WIN_H
                blocks.append(toe_ref[0, k_r - q_r + NA_WIN_H - 1] if inside else masked)
            o_ref[0, v, 0, kr * GRID_W:(kr + 1) * GRID_W, :] = jnp.concatenate(blocks, axis=1)


def _na_bias(rpb):
    sel, ok = _col_tables()
    lh = rpb.shape[0] * HEADS
    pad = jnp.zeros((lh, RPB_ROWS_PAD, RPB_COLS_PAD), F32)
    pad = pad.at[:, :2 * NA_WIN_H - 1, :2 * NA_WIN_W - 1].set(rpb.reshape(lh, 2 * NA_WIN_H - 1, 2 * NA_WIN_W - 1))
    toe = pl.pallas_call(
        _bias_kernel,
        out_shape=jax.ShapeDtypeStruct((lh * RPB_ROWS_PAD, GRID_W * GRID_W), F32),
        compiler_params=pltpu.CompilerParams(vmem_limit_bytes=VMEM_LIMIT_BYTES),
        name="na_bias_columns",
    )(pad.reshape(lh * RPB_ROWS_PAD, RPB_COLS_PAD), sel, ok)
    toe = toe.reshape(lh, RPB_ROWS_PAD, GRID_W, GRID_W)
    return pl.pallas_call(
        _bias_rows_kernel,
        grid=(lh,),
        in_specs=[pl.BlockSpec((1, RPB_ROWS_PAD, GRID_W, GRID_W), lambda n: (n, 0, 0, 0))],
        out_specs=pl.BlockSpec((1, 3, 1, NA_LOCAL_KEYS, TILE), lambda n: (n // HEADS, 0, n % HEADS, 0, 0)),
        out_shape=jax.ShapeDtypeStruct((rpb.shape[0], 3, HEADS, NA_LOCAL_KEYS, TILE), F32),
        compiler_params=_params("arbitrary"),
        name="na_bias_rows",
    )(toe)


def _layer_weights(w_in, w_uq, w_ukv, w_out):
    cuts = np.cumsum([NA_WIDTH] * 4 + [Q_RANK, KV_RANK, MLA_ROPE, MLA_WIDTH])
    na_q, na_k, na_v, na_g, c_q, c_kv, k_r, ml_g = jnp.split(w_in, [int(c) for c in cuts[:-1]], axis=1)
    kr_blk = jnp.concatenate([k_r[:, 0::2], k_r[:, 1::2], jnp.zeros((D_MODEL, HEAD_PAD - MLA_ROPE), F32)], axis=1)
    wtok = jnp.concatenate([na_k, c_q, c_kv, kr_blk], axis=1).astype(BF16)
    wt = jnp.concatenate([na_q, na_v, na_g, ml_g], axis=1).T.astype(BF16)

    uq = w_uq.reshape(Q_RANK, HEADS, MLA_NOPE + MLA_ROPE)
    uq = jnp.concatenate([uq[..., :MLA_NOPE], uq[..., MLA_NOPE::2], uq[..., MLA_NOPE + 1::2],
                          jnp.zeros((Q_RANK, HEADS, HEAD_PAD - MLA_NOPE - MLA_ROPE), F32)], axis=-1)
    wq = uq.reshape(Q_RANK, HEADS * HEAD_PAD).T.astype(BF16)

    ukv = w_ukv.reshape(KV_RANK, HEADS, MLA_NOPE + MLA_V)
    k_top = jnp.concatenate([ukv[..., :MLA_NOPE], jnp.zeros((KV_RANK, HEADS, HEAD_PAD - MLA_NOPE), F32)], axis=-1)
    eye = jnp.concatenate([jnp.zeros((MLA_ROPE, MLA_NOPE), F32), jnp.eye(MLA_ROPE, dtype=F32),
                           jnp.zeros((MLA_ROPE, HEAD_PAD - MLA_NOPE - MLA_ROPE), F32)], axis=1)
    k_bot = jnp.concatenate([jnp.broadcast_to(eye[:, None], (MLA_ROPE, HEADS, HEAD_PAD)),
                             jnp.zeros((KV_RANK - MLA_ROPE, HEADS, HEAD_PAD), F32)], axis=0)
    wk = jnp.concatenate([k_top, k_bot], axis=0).reshape(2 * KV_RANK, HEADS * HEAD_PAD).astype(BF16)
    wv = ukv[..., MLA_NOPE:].reshape(KV_RANK, MLA_WIDTH).T.astype(BF16)

    wa = w_out[:NA_WIDTH].astype(BF16)
    wb = w_out[NA_WIDTH:].astype(BF16)
    return wtok, wt, wq, wk, wv, wa, wb


def kernel(x, c, ctx, c_ctx, norm_g, w_ada, b_ada, w_in, na_rpb, q_norm_g, w_uq, kv_norm_g, w_ukv, w_out,
           final_norm_g):
    b = x.shape[0]
    tabs = _rope_tables()

    rows = -(-(b + 1) // 8) * 8
    cc = jnp.concatenate([c, c_ctx[None], jnp.zeros((rows - b - 1, D_MODEL), F32)], axis=0)
    ada = _ada_call(cc, w_ada, b_ada)
    bias_t = _na_bias(na_rpb)

    def layer_inputs(l):
        mod_lat = ada[l, :b].reshape(b, 1, 3, D_MODEL)
        mod_con = jnp.broadcast_to(ada[l, b].reshape(1, 1, 3, D_MODEL), (b, 1, 3, D_MODEL))
        mod = jnp.concatenate([mod_lat, mod_con], axis=1)
        wtok, wt, wq, wk, wv, wa, wb = _layer_weights(w_in[l], w_uq[l], w_ukv[l], w_out[l])
        proj_args = (mod, norm_g[l].reshape(1, D_MODEL), wtok, wt, q_norm_g[l].reshape(1, Q_RANK), wq,
                     kv_norm_g[l].reshape(1, KV_RANK), wk, wv, *tabs)
        return mod, proj_args, wa, wb

    lat, con = x, ctx
    mod, proj_args, wa, wb = layer_inputs(0)
    proj = _proj_call(lat, con, proj_args)
    for l in range(DEPTH):
        last = l == DEPTH - 1
        kna, qna, vna, gt, qml, kml, vml = proj
        out_tiles = N_LAT_TILES if last else N_TILES
        ona = _na_call(qna, kna, vna, gt, bias_t, l, out_tiles)
        oml = _mla_call(qml, kml, vml, gt)
        if last:
            return _final_call(ona, oml, wa, wb, lat, mod, final_norm_g.reshape(1, D_MODEL))
        nxt_mod, nxt_args, nxt_wa, nxt_wb = layer_inputs(l + 1)
        stream, *proj = _out_proj_call(ona, oml, wa, wb, lat, con, mod, nxt_args)
        lat = con = stream
        mod, wa, wb = nxt_mod, nxt_wa, nxt_wb
```
